```python
import math
import jax, jax.numpy as jnp
from jax import lax
import numpy as np

D_MODEL = 1024
BATCH = 4
SEQ = 4096
DEPTH = 2

N_EVEN = (DEPTH + 1) // 2
N_ODD = DEPTH // 2
MEM_LEN = 256
CONV_DIM = D_MODEL
CONV_WIDTH = 31
CONV_PAD = (CONV_WIDTH - 1) // 2
MLA_HEADS = 8
QK_NOPE = 128
QK_ROPE = 64
V_DIM = 128
Q_LORA = 768
KV_LORA = 256
ROPE_THETA = 10000.0
Q_BLOCK = 128
MEM_HEADS = 4
MEM_HEAD_DIM = 128
MEM_DIM = MEM_HEADS * MEM_HEAD_DIM
S5_DIM = D_MODEL
S5_GROUP = 16
S5_GROUPS = S5_DIM // S5_GROUP
S5_STATE = 64
DT_MIN = 0.001
DT_MAX = 0.1
LN_EPS = 1e-5
RMS_EPS = 1e-6
ALPHA = (2 * DEPTH) ** 0.25
BETA = (8 * DEPTH) ** -0.25

EVEN_SPLITS = (2 * CONV_DIM, CONV_DIM, Q_LORA, KV_LORA, QK_ROPE, MLA_HEADS * V_DIM, MEM_DIM, MEM_DIM)
ODD_SPLITS = (S5_DIM, S5_DIM, MEM_DIM, MEM_DIM)
EVEN_IN = sum(EVEN_SPLITS)
ODD_IN = sum(ODD_SPLITS)
EVEN_MIX = CONV_DIM + MLA_HEADS * V_DIM + MEM_DIM
ODD_MIX = S5_DIM + MEM_DIM

kernel_name = "hybrid_conv_mla_s5_deepnorm_encoder"


def _split(h, sizes):
    idx = np.cumsum(sizes)[:-1].tolist()
    return jnp.split(h, idx, axis=-1)


def layer_norm(x, g, b):
    xf = x.astype(jnp.float32)
    mu = jnp.mean(xf, axis=-1, keepdims=True)
    xc = xf - mu
    var = jnp.mean(xc * xc, axis=-1, keepdims=True)
    return (xc * lax.rsqrt(var + LN_EPS) * g + b).astype(x.dtype)


def rms_norm(x, g):
    xf = x.astype(jnp.float32)
    ms = jnp.mean(xf * xf, axis=-1, keepdims=True)
    return (xf * lax.rsqrt(ms + RMS_EPS) * g).astype(x.dtype)


def rope_tables(positions):
    inv_freq = ROPE_THETA ** (-jnp.arange(0, QK_ROPE, 2, dtype=jnp.float32) / QK_ROPE)
    ang = positions.astype(jnp.float32)[..., None] * inv_freq
    return jnp.cos(ang), jnp.sin(ang)


def apply_rope(x, cos, sin):
    x1, x2 = jnp.split(x, 2, axis=-1)
    out = jnp.concatenate([x1 * cos - x2 * sin, x2 * cos + x1 * sin], axis=-1)
    return out.astype(x.dtype)


def mla_block_attention(q_nope, q_rope, k_nope, k_rope, v):
    bsz, seq, heads, _ = q_nope.shape
    nb = seq // Q_BLOCK
    scale = (QK_NOPE + QK_ROPE) ** -0.5
    qn = q_nope.reshape(bsz, nb, Q_BLOCK, heads, QK_NOPE).transpose(1, 0, 2, 3, 4)
    qr = q_rope.reshape(bsz, nb, Q_BLOCK, heads, QK_ROPE).transpose(1, 0, 2, 3, 4)

    def one_block(args):
        qn_b, qr_b = args
        s = (jnp.einsum('bqhd,bkhd->bhqk', qn_b, k_nope).astype(jnp.float32)
             + jnp.einsum('bqhr,bkr->bhqk', qr_b, k_rope).astype(jnp.float32)) * scale
        p = jax.nn.softmax(s, axis=-1).astype(v.dtype)
        return jnp.einsum('bhqk,bkhd->bqhd', p, v)

    o = lax.map(one_block, (qn, qr))
    return o.transpose(1, 0, 2, 3, 4).reshape(bsz, seq, heads * V_DIM)


def memory_attention(q, mem, w_mem_kv):
    bsz, seq, _ = q.shape
    kv = mem @ w_mem_kv
    k, v = jnp.split(kv, 2, axis=-1)
    k = k.reshape(bsz, -1, MEM_HEADS, MEM_HEAD_DIM)
    v = v.reshape(bsz, -1, MEM_HEADS, MEM_HEAD_DIM)
    qh = q.reshape(bsz, seq, MEM_HEADS, MEM_HEAD_DIM)
    s = jnp.einsum('bshd,bmhd->bhsm', qh, k).astype(jnp.float32) * (MEM_HEAD_DIM ** -0.5)
    p = jax.nn.softmax(s, axis=-1).astype(v.dtype)
    return jnp.einsum('bhsm,bmhd->bshd', p, v).reshape(bsz, seq, MEM_DIM)


def _cplx_combine(e1, e2):
    ar1, ai1, br1, bi1 = e1
    ar2, ai2, br2, bi2 = e2
    return (ar2 * ar1 - ai2 * ai1,
            ar2 * ai1 + ai2 * ar1,
            ar2 * br1 - ai2 * bi1 + br2,
            ar2 * bi1 + ai2 * br1 + bi2)


def s5_direction(u, a_re, a_im, log_dt, b_re, b_im, c_re, c_im, reverse):
    f32 = jnp.float32
    lam_re = jnp.minimum(a_re.astype(f32), -1e-4)
    lam_im = a_im.astype(f32)
    dt = jnp.exp(log_dt.astype(f32))[:, None]
    mag = jnp.exp(lam_re * dt)
    lb_re = mag * jnp.cos(lam_im * dt)
    lb_im = mag * jnp.sin(lam_im * dt)
    den = lam_re * lam_re + lam_im * lam_im
    nr = lb_re - 1.0
    f_re = (nr * lam_re + lb_im * lam_im) / den
    f_im = (lb_im * lam_re - nr * lam_im) / den
    br = b_re.astype(f32)
    bi = b_im.astype(f32)
    bb_re = f_re[..., None] * br - f_im[..., None] * bi
    bb_im = f_re[..., None] * bi + f_im[..., None] * br
    bu_re = jnp.einsum('bsgc,gpc->bsgp', u, bb_re)
    bu_im = jnp.einsum('bsgc,gpc->bsgp', u, bb_im)
    seq = u.shape[1]
    a_re_seq = jnp.broadcast_to(lb_re, (1, seq) + lb_re.shape)
    a_im_seq = jnp.broadcast_to(lb_im, (1, seq) + lb_im.shape)
    _, _, x_re, x_im = lax.associative_scan(
        _cplx_combine, (a_re_seq, a_im_seq, bu_re, bu_im), reverse=reverse, axis=1)
    return (jnp.einsum('bsgp,gcp->bsgc', x_re, c_re.astype(f32))
            - jnp.einsum('bsgp,gcp->bsgc', x_im, c_im.astype(f32)))


def even_layer(x, mem, cos, sin, w_in, conv_w, conv_b, conv_ln_g, conv_ln_b,
               q_norm, w_uq, kv_norm, w_ukv, w_mem_kv, w_out, ln_g, ln_b):
    bsz, seq, _ = x.shape
    h = x @ w_in
    conv_in, conv_gate, c_q, c_kv, k_rope, mla_gate, mem_q, mem_gate = _split(h, EVEN_SPLITS)

    glu = conv_in[..., :CONV_DIM] * jax.nn.sigmoid(conv_in[..., CONV_DIM:])
    dw = lax.conv_general_dilated(
        glu, conv_w, window_strides=(1,), padding=[(CONV_PAD, CONV_PAD)],
        dimension_numbers=('NWC', 'WIO', 'NWC'), feature_group_count=CONV_DIM) + conv_b
    a_out = jax.nn.silu(layer_norm(dw, conv_ln_g, conv_ln_b)) * jax.nn.silu(conv_gate)

    q = (rms_norm(c_q, q_norm) @ w_uq).reshape(bsz, seq, MLA_HEADS, QK_NOPE + QK_ROPE)
    q_nope, q_rope = q[..., :QK_NOPE], q[..., QK_NOPE:]
    q_rope = apply_rope(q_rope, cos[:, :, None, :], sin[:, :, None, :])
    kv = (rms_norm(c_kv, kv_norm) @ w_ukv).reshape(bsz, seq, MLA_HEADS, QK_NOPE + V_DIM)
    k_nope, v = kv[..., :QK_NOPE], kv[..., QK_NOPE:]
    k_rope = apply_rope(k_rope, cos, sin)
    b_out = mla_block_attention(q_nope, q_rope, k_nope, k_rope, v) * jax.nn.silu(mla_gate)

    m_out = memory_attention(mem_q, mem, w_mem_kv) * jax.nn.silu(mem_gate)

    y = jnp.concatenate([a_out, b_out, m_out], axis=-1) @ w_out
    return layer_norm(ALPHA * x + y, ln_g, ln_b)


def odd_layer(x, mem, w_in, s5_fwd, s5_bwd, s5_d, w_glu, w_mem_kv, w_out, ln_g, ln_b):
    bsz, seq, _ = x.shape
    h = x @ w_in
    s5_u, s5_gate, mem_q, mem_gate = _split(h, ODD_SPLITS)

    u = s5_u.astype(jnp.float32).reshape(bsz, seq, S5_GROUPS, S5_GROUP)
    y = s5_direction(u, *s5_fwd, reverse=False) + s5_direction(u, *s5_bwd, reverse=True)
    y = y.reshape(bsz, seq, S5_DIM) + s5_d.astype(jnp.float32) * s5_u.astype(jnp.float32)
    z = jax.nn.gelu(y).astype(x.dtype) @ w_glu
    c_out = (z[..., :S5_DIM] * jax.nn.sigmoid(z[..., S5_DIM:])) * jax.nn.silu(s5_gate)

    m_out = memory_attention(mem_q, mem, w_mem_kv) * jax.nn.silu(mem_gate)

    out = jnp.concatenate([c_out, m_out], axis=-1) @ w_out
    return layer_norm(ALPHA * x + out, ln_g, ln_b)


def setup_inputs(seed: int = 0) -> dict:
    key = jax.random.key(seed)
    ks = iter(jax.random.split(key, 48))
    f32 = jnp.float32

    def nrm(shape, std):
        return jax.random.normal(next(ks), shape, f32) * std

    def gain(shape):
        return 1.0 + nrm(shape, 0.01)

    x = nrm((BATCH, SEQ, D_MODEL), 1.0)
    mem = nrm((BATCH, MEM_LEN, D_MODEL), 1.0)
    offset = jax.random.randint(next(ks), (BATCH, 1), 0, SEQ, dtype=jnp.int32)
    positions = (jnp.arange(SEQ, dtype=jnp.int32)[None, :] + offset).astype(jnp.int32)

    E, O, G, P, C = N_EVEN, N_ODD, S5_GROUPS, S5_STATE, S5_GROUP
    inp = {
        "x": x, "mem": mem, "positions": positions,
        "e_w_in": nrm((E, D_MODEL, EVEN_IN), D_MODEL ** -0.5),
        "e_conv_w": nrm((E, CONV_WIDTH, 1, CONV_DIM), CONV_WIDTH ** -0.5),
        "e_conv_b": nrm((E, CONV_DIM), 0.01),
        "e_conv_ln_g": gain((E, CONV_DIM)),
        "e_conv_ln_b": nrm((E, CONV_DIM), 0.01),
        "e_q_norm": gain((E, Q_LORA)),
        "e_w_uq": nrm((E, Q_LORA, MLA_HEADS * (QK_NOPE + QK_ROPE)), Q_LORA ** -0.5),
        "e_kv_norm": gain((E, KV_LORA)),
        "e_w_ukv": nrm((E, KV_LORA, MLA_HEADS * (QK_NOPE + V_DIM)), KV_LORA ** -0.5),
        "e_mem_kv": nrm((E, D_MODEL, 2 * MEM_DIM), D_MODEL ** -0.5),
        "e_w_out": nrm((E, EVEN_MIX, D_MODEL), BETA * EVEN_MIX ** -0.5),
        "e_ln_g": gain((E, D_MODEL)),
        "e_ln_b": nrm((E, D_MODEL), 0.01),
        "o_w_in": nrm((O, D_MODEL, ODD_IN), D_MODEL ** -0.5),
    }
    a_im_init = jnp.broadcast_to(math.pi * jnp.arange(P, dtype=f32), (O, G, P))
    for d in ("f", "b"):
        inp["o_a_re_" + d] = -0.5 + nrm((O, G, P), 0.01)
        inp["o_a_im_" + d] = a_im_init + nrm((O, G, P), 0.01)
        inp["o_log_dt_" + d] = jax.random.uniform(next(ks), (O, G), f32,
                                                  math.log(DT_MIN), math.log(DT_MAX))
        inp["o_b_re_" + d] = nrm((O, G, P, C), (2.0 * C) ** -0.5)
        inp["o_b_im_" + d] = nrm((O, G, P, C), (2.0 * C) ** -0.5)
        inp["o_c_re_" + d] = nrm((O, G, C, P), (2.0 * P) ** -0.5)
        inp["o_c_im_" + d] = nrm((O, G, C, P), (2.0 * P) ** -0.5)
    inp["o_d"] = nrm((O, S5_DIM), 1.0)
    inp["o_w_glu"] = nrm((O, S5_DIM, 2 * S5_DIM), S5_DIM ** -0.5)
    inp["o_mem_kv"] = nrm((O, D_MODEL, 2 * MEM_DIM), D_MODEL ** -0.5)
    inp["o_w_out"] = nrm((O, ODD_MIX, D_MODEL), BETA * ODD_MIX ** -0.5)
    inp["o_ln_g"] = gain((O, D_MODEL))
    inp["o_ln_b"] = nrm((O, D_MODEL), 0.01)
    return inp


def reference(x, mem, positions,
              e_w_in, e_conv_w, e_conv_b, e_conv_ln_g, e_conv_ln_b, e_q_norm, e_w_uq,
              e_kv_norm, e_w_ukv, e_mem_kv, e_w_out, e_ln_g, e_ln_b,
              o_w_in,
              o_a_re_f, o_a_im_f, o_log_dt_f, o_b_re_f, o_b_im_f, o_c_re_f, o_c_im_f,
              o_a_re_b, o_a_im_b, o_log_dt_b, o_b_re_b, o_b_im_b, o_c_re_b, o_c_im_b,
              o_d, o_w_glu, o_mem_kv, o_w_out, o_ln_g, o_ln_b):
    cos, sin = rope_tables(positions)
    h = x
    for layer in range(DEPTH):
        i = layer // 2
        if layer % 2 == 0:
            h = even_layer(h, mem, cos, sin, e_w_in[i], e_conv_w[i], e_conv_b[i],
                           e_conv_ln_g[i], e_conv_ln_b[i], e_q_norm[i], e_w_uq[i],
                           e_kv_norm[i], e_w_ukv[i], e_mem_kv[i], e_w_out[i],
                           e_ln_g[i], e_ln_b[i])
        else:
            s5_fwd = (o_a_re_f[i], o_a_im_f[i], o_log_dt_f[i], o_b_re_f[i], o_b_im_f[i],
                      o_c_re_f[i], o_c_im_f[i])
            s5_bwd = (o_a_re_b[i], o_a_im_b[i], o_log_dt_b[i], o_b_re_b[i], o_b_im_b[i],
                      o_c_re_b[i], o_c_im_b[i])
            h = odd_layer(h, mem, o_w_in[i], s5_fwd, s5_bwd, o_d[i], o_w_glu[i],
                          o_mem_kv[i], o_w_out[i], o_ln_g[i], o_ln_b[i])
    return h
```

```python
import functools
import math

import jax
import jax.numpy as jnp
from jax import lax
from jax.experimental import pallas as pl
from jax.experimental.pallas import tpu as pltpu

F32 = jnp.float32
BF16 = jnp.bfloat16
HI = lax.Precision.HIGHEST

D_MODEL = 1024
CONV_DIM = 1024
CONV_WIDTH = 31
CONV_PAD = 15
MLA_HEADS = 8
QK_NOPE = 128
QK_ROPE = 64
V_DIM = 128
Q_LORA = 768
KV_LORA = 256
ROPE_THETA = 10000.0
MEM_HEADS = 4
MEM_HEAD_DIM = 128
MEM_DIM = 512
S5_DIM = 1024
S5_GROUP = 16
S5_GROUPS = 64
S5_STATE = 64
LN_EPS = 1e-5
RMS_EPS = 1e-6
DEPTH = 2
ALPHA = (2 * DEPTH) ** 0.25

LANE = 128
VMEM_LIMIT = 56 * 1024 * 1024

E_CONV_IN = 0
E_CONV_GATE = 2048
E_CQ = 3072
E_CKV = 3840
E_MLA_GATE = 4096
E_MEM_Q = 5120
E_MEM_GATE = 5632
E_KROPE = 6144
E_NPAD = 6656
O_U = 0
O_GATE = 1024
O_MEM_Q = 2048
O_MEM_GATE = 2560
O_N = 3072

S5_L = 8


def _cparams(sem):
    return pltpu.CompilerParams(dimension_semantics=sem, vmem_limit_bytes=VMEM_LIMIT)


def _silu(x):
    return x * jax.nn.sigmoid(x)


def _mm_body(a_ref, w_ref, o_ref):
    o_ref[...] = jnp.dot(a_ref[...].astype(BF16), w_ref[...],
                         preferred_element_type=F32).astype(o_ref.dtype)


def _matmul(a, w, out_dtype, tm, tn, name):
    m, k = a.shape
    n = w.shape[1]
    tm = min(tm, m)
    assert m % tm == 0 and n % tn == 0
    return pl.pallas_call(
        _mm_body,
        grid=(m // tm, n // tn),
        in_specs=[pl.BlockSpec((tm, k), lambda i, j: (i, 0)),
                  pl.BlockSpec((k, tn), lambda i, j: (0, j))],
        out_specs=pl.BlockSpec((tm, tn), lambda i, j: (i, j)),
        out_shape=jax.ShapeDtypeStruct((m, n), out_dtype),
        compiler_params=_cparams(("parallel", "parallel")),
        name=name,
    )(a, w)


CONV_HALO = 16
CONV_RC = 32


def _conv_body(m1, m2, p1, p2, n1, n2, gate, cw, cb, lg, lb, o_ref, slab, *, ts, tiles_per_seq):
    i = pl.program_id(0)
    first = (i % tiles_per_seq) == 0
    last = (i % tiles_per_seq) == tiles_per_seq - 1

    def glu(a, b):
        return a * jax.nn.sigmoid(b)

    slab[pl.ds(CONV_HALO, ts), :] = glu(m1[...], m2[...])
    slab[pl.ds(0, CONV_HALO), :] = jnp.where(first, 0.0, glu(p1[...], p2[...]))
    slab[pl.ds(CONV_HALO + ts, CONV_HALO), :] = jnp.where(last, 0.0, glu(n1[...], n2[...]))

    off = CONV_HALO - CONV_PAD
    for r in range(ts // CONV_RC):
        base = r * CONV_RC
        acc = jnp.zeros((CONV_RC, CONV_DIM), F32)
        for k in range(CONV_WIDTH):
            acc = acc + slab[pl.ds(base + off + k, CONV_RC), :] * cw[pl.ds(k, 1), :]
        dw = acc + cb[...]
        mu = jnp.mean(dw, axis=-1, keepdims=True)
        xc = dw - mu
        var = jnp.mean(xc * xc, axis=-1, keepdims=True)
        y = xc * lax.rsqrt(var + LN_EPS) * lg[...] + lb[...]
        res = _silu(y) * _silu(gate[pl.ds(base, CONV_RC), :])
        o_ref[pl.ds(base, CONV_RC), :] = res.astype(o_ref.dtype)


def _conv_branch(h, conv_w, conv_b, ln_g, ln_b, seq, ts=256):
    t = h.shape[0]
    nb = ts // CONV_HALO
    last_blk = t // CONV_HALO - 1
    c1, c2, cg = E_CONV_IN // CONV_DIM, E_CONV_IN // CONV_DIM + 1, E_CONV_GATE // CONV_DIM
    body = functools.partial(_conv_body, ts=ts, tiles_per_seq=seq // ts)
    vec = pl.BlockSpec((1, CONV_DIM), lambda i: (0, 0))
    return pl.pallas_call(
        body,
        grid=(t // ts,),
        in_specs=[
            pl.BlockSpec((ts, CONV_DIM), lambda i: (i, c1)),
            pl.BlockSpec((ts, CONV_DIM), lambda i: (i, c2)),
            pl.BlockSpec((CONV_HALO, CONV_DIM), lambda i: (jnp.maximum(i * nb - 1, 0), c1)),
            pl.BlockSpec((CONV_HALO, CONV_DIM), lambda i: (jnp.maximum(i * nb - 1, 0), c2)),
            pl.BlockSpec((CONV_HALO, CONV_DIM), lambda i: (jnp.minimum((i + 1) * nb, last_blk), c1)),
            pl.BlockSpec((CONV_HALO, CONV_DIM), lambda i: (jnp.minimum((i + 1) * nb, last_blk), c2)),
            pl.BlockSpec((ts, CONV_DIM), lambda i: (i, cg)),
            pl.BlockSpec((CONV_WIDTH, CONV_DIM), lambda i: (0, 0)),
            vec, vec, vec,
        ],
        out_specs=pl.BlockSpec((ts, CONV_DIM), lambda i: (i, 0)),
        out_shape=jax.ShapeDtypeStruct((t, CONV_DIM), BF16),
        scratch_shapes=[pltpu.VMEM((ts + 2 * CONV_HALO, CONV_DIM), F32)],
        compiler_params=_cparams(("parallel",)),
        name="conv_branch",
    )(h, h, h, h, h, h, h, conv_w, conv_b, ln_g, ln_b)


def _rms(x, g):
    ms = jnp.mean(x * x, axis=-1, keepdims=True)
    return x * lax.rsqrt(ms + RMS_EPS) * g


def _mla_proj_body(cq, ckv, kr, cos, sin, qn_g, kvn_g, wq, wkv, q_ref, k_ref, v_ref, *, scale):
    c = cos[0]
    s = sin[0]
    nq = _rms(cq[...], qn_g[...]).astype(BF16)
    qf = jnp.dot(nq, wq[...], preferred_element_type=F32)
    nh = MLA_HEADS * QK_NOPE
    nr = MLA_HEADS * QK_ROPE
    c4 = jnp.concatenate([c] * (nr // LANE), axis=-1)
    s4 = jnp.concatenate([s] * (nr // LANE), axis=-1)
    qr = qf[:, nh:nh + nr] * c4 + qf[:, nh + nr:nh + 2 * nr] * s4
    nkv = _rms(ckv[...], kvn_g[...]).astype(BF16)
    kvf = jnp.dot(nkv, wkv[...], preferred_element_type=F32)
    krf = kr[...]
    kr_even = krf[:, 0:LANE] * c + krf[:, 2 * LANE:3 * LANE] * s
    kr_odd = krf[:, LANE:2 * LANE] * c + krf[:, 3 * LANE:4 * LANE] * s
    for h in range(MLA_HEADS):
        q_ref[0, h, :, 0:QK_NOPE] = (qf[:, h * QK_NOPE:(h + 1) * QK_NOPE] * scale).astype(BF16)
        p = h // 2
        q_ref[0, h, :, QK_NOPE:2 * QK_NOPE] = (qr[:, p * LANE:(p + 1) * LANE] * scale).astype(BF16)
        k_ref[0, h, :, 0:QK_NOPE] = kvf[:, h * QK_NOPE:(h + 1) * QK_NOPE].astype(BF16)
        k_ref[0, h, :, QK_NOPE:2 * QK_NOPE] = (kr_even if h % 2 == 0 else kr_odd).astype(BF16)
        v_ref[0, h, :, :] = kvf[:, nh + h * V_DIM:nh + (h + 1) * V_DIM].astype(BF16)


def _mla_proj(h, cos128, sin128, q_norm, kv_norm, wq, wkv, bsz, seq, tm=512):
    nt = seq // tm
    scale = (QK_NOPE + QK_ROPE) ** -0.5
    body = functools.partial(_mla_proj_body, scale=scale)
    hd = 2 * QK_NOPE
    return pl.pallas_call(
        body,
        grid=(bsz, nt),
        in_specs=[
            pl.BlockSpec((tm, Q_LORA), lambda b, i: (b * nt + i, E_CQ // Q_LORA)),
            pl.BlockSpec((tm, KV_LORA), lambda b, i: (b * nt + i, E_CKV // KV_LORA)),
            pl.BlockSpec((tm, 4 * LANE), lambda b, i: (b * nt + i, E_KROPE // (4 * LANE))),
            pl.BlockSpec((1, tm, LANE), lambda b, i: (b, i, 0)),
            pl.BlockSpec((1, tm, LANE), lambda b, i: (b, i, 0)),
            pl.BlockSpec((1, Q_LORA), lambda b, i: (0, 0)),
            pl.BlockSpec((1, KV_LORA), lambda b, i: (0, 0)),
            pl.BlockSpec(wq.shape, lambda b, i: (0, 0)),
            pl.BlockSpec(wkv.shape, lambda b, i: (0, 0)),
        ],
        out_specs=[
            pl.BlockSpec((1, MLA_HEADS, tm, hd), lambda b, i: (b, 0, i, 0)),
            pl.BlockSpec((1, MLA_HEADS, tm, hd), lambda b, i: (b, 0, i, 0)),
            pl.BlockSpec((1, MLA_HEADS, tm, V_DIM), lambda b, i: (b, 0, i, 0)),
        ],
        out_shape=[
            jax.ShapeDtypeStruct((bsz, MLA_HEADS, seq, hd), BF16),
            jax.ShapeDtypeStruct((bsz, MLA_HEADS, seq, hd), BF16),
            jax.ShapeDtypeStruct((bsz, MLA_HEADS, seq, V_DIM), BF16),
        ],
        compiler_params=_cparams(("parallel", "parallel")),
        name="mla_proj",
    )(h, h, h, cos128, sin128, q_norm, kv_norm, wq, wkv)


def _flash_body(q_ref, k_ref, v_ref, g_ref, o_ref, *, tk, nk):
    q = q_ref[0, 0]
    tq = q.shape[0]

    def step(c, carry):
        m, l, acc = carry
        start = pl.multiple_of(c * tk, tk)
        ks = k_ref[0, 0, pl.ds(start, tk), :]
        vs = v_ref[0, 0, pl.ds(start, tk), :]
        s = lax.dot_general(q, ks, (((1,), (1,)), ((), ())), preferred_element_type=F32)
        m_new = jnp.maximum(m, jnp.max(s, axis=-1, keepdims=True))
        alpha = jnp.exp(m - m_new)
        p = jnp.exp(s - m_new)
        l = alpha * l + jnp.sum(p, axis=-1, keepdims=True)
        acc = alpha * acc + jnp.dot(p.astype(BF16), vs, preferred_element_type=F32)
        return m_new, l, acc

    m0 = jnp.full((tq, 1), -jnp.inf, F32)
    l0 = jnp.zeros((tq, 1), F32)
    a0 = jnp.zeros((tq, V_DIM), F32)
    m, l, acc = lax.fori_loop(0, nk, step, (m0, l0, a0))
    o_ref[...] = (acc / l * _silu(g_ref[...])).astype(o_ref.dtype)


def _mla_attention(q, k, v, h, tq=512, tk=512):
    bsz, heads, seq, hd = q.shape
    nq = seq // tq
    body = functools.partial(_flash_body, tk=tk, nk=seq // tk)
    gcol = E_MLA_GATE // V_DIM
    return pl.pallas_call(
        body,
        grid=(bsz, heads, nq),
        in_specs=[
            pl.BlockSpec((1, 1, tq, hd), lambda b, hh, i: (b, hh, i, 0)),
            pl.BlockSpec((1, 1, seq, hd), lambda b, hh, i: (b, hh, 0, 0)),
            pl.BlockSpec((1, 1, seq, V_DIM), lambda b, hh, i: (b, hh, 0, 0)),
            pl.BlockSpec((tq, V_DIM), lambda b, hh, i: (b * nq + i, gcol + hh)),
        ],
        out_specs=pl.BlockSpec((tq, V_DIM), lambda b, hh, i: (b * nq + i, hh)),
        out_shape=jax.ShapeDtypeStruct((bsz * seq, heads * V_DIM), BF16),
        compiler_params=_cparams(("parallel", "parallel", "arbitrary")),
        name="mla_attention",
    )(q, k, v, h)


def _mem_attn_body(q_ref, g_ref, kv_ref, o_ref, *, scale):
    for hh in range(MEM_HEADS):
        lo, hi = hh * MEM_HEAD_DIM, (hh + 1) * MEM_HEAD_DIM
        q = (q_ref[:, lo:hi] * scale).astype(BF16)
        kk = kv_ref[0, :, lo:hi]
        vv = kv_ref[0, :, MEM_DIM + lo:MEM_DIM + hi]
        s = lax.dot_general(q, kk, (((1,), (1,)), ((), ())), preferred_element_type=F32)
        m = jnp.max(s, axis=-1, keepdims=True)
        p = jnp.exp(s - m)
        l = jnp.sum(p, axis=-1, keepdims=True)
        o = jnp.dot(p.astype(BF16), vv, preferred_element_type=F32) / l
        o_ref[:, lo:hi] = (o * _silu(g_ref[:, lo:hi])).astype(o_ref.dtype)


def _mem_attention(h, memkv, q_off, g_off, seq, tm=512):
    t = h.shape[0]
    nt = seq // tm
    mlen = memkv.shape[1]
    body = functools.partial(_mem_attn_body, scale=MEM_HEAD_DIM ** -0.5)
    return pl.pallas_call(
        body,
        grid=(t // tm,),
        in_specs=[
            pl.BlockSpec((tm, MEM_DIM), lambda i: (i, q_off // MEM_DIM)),
            pl.BlockSpec((tm, MEM_DIM), lambda i: (i, g_off // MEM_DIM)),
            pl.BlockSpec((1, mlen, 2 * MEM_DIM), lambda i: (i // nt, 0, 0)),
        ],
        out_specs=pl.BlockSpec((tm, MEM_DIM), lambda i: (i, 0)),
        out_shape=jax.ShapeDtypeStruct((t, MEM_DIM), BF16),
        compiler_params=_cparams(("parallel",)),
        name="mem_attention",
    )(h, h, memkv)


def _out_ln_body(*refs, nparts):
    parts = refs[:nparts]
    ws = refs[nparts:2 * nparts]
    x_ref, g_ref, b_ref, o_ref = refs[2 * nparts:]
    y = jnp.dot(parts[0][...], ws[0][...], preferred_element_type=F32)
    for p, w in zip(parts[1:], ws[1:]):
        y = y + jnp.dot(p[...], w[...], preferred_element_type=F32)
    z = ALPHA * x_ref[...] + y
    mu = jnp.mean(z, axis=-1, keepdims=True)
    zc = z - mu
    var = jnp.mean(zc * zc, axis=-1, keepdims=True)
    o_ref[...] = zc * lax.rsqrt(var + LN_EPS) * g_ref[...] + b_ref[...]


def _out_ln(parts, ws, x, g, b, tm=512, name="out_ln"):
    t = x.shape[0]
    n = len(parts)
    body = functools.partial(_out_ln_body, nparts=n)
    in_specs = [pl.BlockSpec((tm, p.shape[1]), lambda i: (i, 0)) for p in parts]
    in_specs += [pl.BlockSpec(w.shape, lambda i: (0, 0)) for w in ws]
    in_specs += [pl.BlockSpec((tm, D_MODEL), lambda i: (i, 0)),
                 pl.BlockSpec((1, D_MODEL), lambda i: (0, 0)),
                 pl.BlockSpec((1, D_MODEL), lambda i: (0, 0))]
    return pl.pallas_call(
        body,
        grid=(t // tm,),
        in_specs=in_specs,
        out_specs=pl.BlockSpec((tm, D_MODEL), lambda i: (i, 0)),
        out_shape=jax.ShapeDtypeStruct((t, D_MODEL), F32),
        compiler_params=_cparams(("parallel",)),
        name=name,
    )(*parts, *ws, x, g, b)


S5_ROWS_PER_B = 8
PW_A8, PW_A64, PW_A512, PW_A1024, PW_A2048 = 0, 1, 2, 3, 4
PW_A64K = 5
PW_A8K = 13
PW_ROWS = 24


def _cmul(pr, pi, xr, xi):
    return pr * xr - pi * xi, pr * xi + pi * xr


def _s5_scan(s_ref, x_ref, pw_ref, rev, bsz):
    half = 512
    n0 = 8 * bsz * S5_ROWS_PER_B
    n1 = bsz * S5_ROWS_PER_B

    def pw(row):
        v = pw_ref[0, pl.ds(row, 1), :]
        return v[:, :half], v[:, half:]

    def blk0(k):
        return (7 - k if rev else k) * n0

    def blk1(k):
        return (7 - k if rev else k) * n1

    def ld(ref, start, n):
        v = ref[pl.ds(start, n), :]
        return v[:, :half], v[:, half:]

    def st(ref, start, n, re, im):
        ref[pl.ds(start, n), 0:half] = re
        ref[pl.ds(start, n), half:2 * half] = im

    a8r, a8i = pw(PW_A8)
    for k in range(1, 8):
        pr_, pi_ = ld(s_ref, blk0(k - 1), n0)
        cr, ci = ld(s_ref, blk0(k), n0)
        mr, mi = _cmul(a8r, a8i, pr_, pi_)
        st(s_ref, blk0(k), n0, cr + mr, ci + mi)
    g0 = blk0(7)
    a64r, a64i = pw(PW_A64)
    for k in range(1, 8):
        pr_, pi_ = ld(s_ref, g0 + blk1(k - 1), n1)
        cr, ci = ld(s_ref, g0 + blk1(k), n1)
        mr, mi = _cmul(a64r, a64i, pr_, pi_)
        st(s_ref, g0 + blk1(k), n1, cr + mr, ci + mi)
    hr, hi = ld(s_ref, g0 + blk1(7), n1)
    c2 = lax.broadcasted_iota(jnp.int32, (n1, half), 0) % S5_ROWS_PER_B
    if rev:
        c2 = (S5_ROWS_PER_B - 1) - c2
    for d, row in ((1, PW_A512), (2, PW_A1024), (4, PW_A2048)):
        ar, ai = pw(row)
        sh = (n1 - d) if rev else d
        sr = pltpu.roll(hr, sh, 0)
        si = pltpu.roll(hi, sh, 0)
        mr, mi = _cmul(ar, ai, sr, si)
        keep = c2 >= d
        hr = hr + jnp.where(keep, mr, 0.0)
        hi = hi + jnp.where(keep, mi, 0.0)
    sh1 = (n1 - 1) if rev else 1
    p3r = jnp.where(c2 >= 1, pltpu.roll(hr, sh1, 0), 0.0)
    p3i = jnp.where(c2 >= 1, pltpu.roll(hi, sh1, 0), 0.0)
    for k1 in range(8):
        if k1 == 0:
            er, ei = p3r, p3i
        else:
            ar, ai = pw(PW_A64K + k1 - 1)
            mr, mi = _cmul(ar, ai, p3r, p3i)
            qr, qi = ld(s_ref, g0 + blk1(k1 - 1), n1)
            er, ei = qr + mr, qi + mi
        for k0 in range(8):
            dst = blk0(k0) + blk1(k1)
            if k0 == 0:
                st(x_ref, dst, n1, er, ei)
            else:
                ar, ai = pw(PW_A8K + k0)
                mr, mi = _cmul(ar, ai, er, ei)
                qr, qi = ld(s_ref, blk0(k0 - 1) + blk1(k1), n1)
                st(x_ref, dst, n1, qr + mr, qi + mi)


def _s5_body(u_ref, tw_ref, wf_ref, wb_ref, vf_ref, vb_ref, pwf_ref, pwb_ref, d_ref, o_ref,
             xcat, sbuf, xf, xb, *, seq):
    n1 = S5_ROWS_PER_B
    tok_stride = seq // S5_ROWS_PER_B

    def pieces():
        for c0 in range(8):
            for c1 in range(8):
                for l in range(S5_L):
                    yield (c0 * 8 * n1 + c1 * n1, l * LANE,
                           pl.ds(c1 * 64 + c0 * 8 + l, n1, stride=tok_stride))

    for r0, l0, tok in pieces():
        xcat[pl.ds(r0, n1), l0:l0 + LANE] = u_ref[tok, :]
    xb16 = xcat[...].astype(BF16)
    sbuf[...] = jnp.dot(xb16, wf_ref[0], preferred_element_type=F32)
    _s5_scan(sbuf, xf, pwf_ref, False, 1)
    sbuf[...] = jnp.dot(xb16, wb_ref[0], preferred_element_type=F32)
    _s5_scan(sbuf, xb, pwb_ref, True, 1)
    y = jnp.dot(xb16, tw_ref[0], preferred_element_type=F32)
    y = y + jnp.dot(xf[...].astype(BF16), vf_ref[0], preferred_element_type=F32)
    y = y + jnp.dot(xb[...].astype(BF16), vb_ref[0], preferred_element_type=F32)
    dd = jnp.concatenate([d_ref[...]] * S5_L, axis=-1)
    sbuf[...] = jax.nn.gelu(y + dd * xcat[...])
    for r0, l0, tok in pieces():
        o_ref[tok, :] = sbuf[pl.ds(r0, n1), l0:l0 + LANE].astype(o_ref.dtype)


def _s5_branch(h2, mats, d, bsz, seq):
    t = h2.shape[0]
    tw, wf, wb, vf, vb, pwf, pwb = mats
    ntile = S5_DIM // LANE
    rows = seq // S5_L
    wide = S5_L * LANE
    body = functools.partial(_s5_body, seq=seq)
    mat = pl.BlockSpec((1, wide, wide), lambda j, b: (j, 0, 0))
    pws = pl.BlockSpec((1, PW_ROWS, wide), lambda j, b: (j, 0, 0))
    return pl.pallas_call(
        body,
        grid=(ntile, bsz),
        in_specs=[pl.BlockSpec((seq, LANE), lambda j, b: (b, O_U // LANE + j)),
                  mat, mat, mat, mat, mat, pws, pws,
                  pl.BlockSpec((1, LANE), lambda j, b: (0, j))],
        out_specs=pl.BlockSpec((seq, LANE), lambda j, b: (b, j)),
        out_shape=jax.ShapeDtypeStruct((t, S5_DIM), F32),
        scratch_shapes=[pltpu.VMEM((rows, wide), F32)] * 4,
        compiler_params=_cparams(("parallel", "parallel")),
        name="s5_branch",
    )(h2, tw, wf, wb, vf, vb, pwf, pwb, d)


def _s5_dir_params(a_re, a_im, log_dt, b_re, b_im, c_re, c_im):
    lam_re = jnp.minimum(a_re, -1e-4)
    lam_im = a_im
    dt = jnp.exp(log_dt)[:, None]
    mag = jnp.exp(lam_re * dt)
    lb_re = mag * jnp.cos(lam_im * dt)
    lb_im = mag * jnp.sin(lam_im * dt)
    den = lam_re * lam_re + lam_im * lam_im
    nr = lb_re - 1.0
    f_re = (nr * lam_re + lb_im * lam_im) / den
    f_im = (lb_im * lam_re - nr * lam_im) / den
    bb_re = f_re[..., None] * b_re - f_im[..., None] * b_im
    bb_im = f_re[..., None] * b_im + f_im[..., None] * b_re

    def power(n):
        n = jnp.asarray(n, F32)
        shape = n.shape + (1, 1)
        n = n.reshape(shape)
        m = jnp.exp(n * (lam_re * dt))
        ang = n * (lam_im * dt)
        return m * jnp.cos(ang), m * jnp.sin(ang)

    return bb_re, bb_im, c_re, c_im, power


def _s5_mats(fwd, bwd):
    G, P, C, L = S5_GROUPS, S5_STATE, S5_GROUP, S5_L
    nt = G // 8
    eye = jnp.eye(8, dtype=F32)
    outs = {}
    kern = {}
    for name, prm in (("f", fwd), ("b", bwd)):
        bb_re, bb_im, c_re, c_im, power = _s5_dir_params(*prm)
        pr, pi = power(jnp.arange(L + 1))
        ca_re = c_re[None] * pr[:, :, None, :] - c_im[None] * pi[:, :, None, :]
        ca_im = c_re[None] * pi[:, :, None, :] + c_im[None] * pr[:, :, None, :]
        kern[name] = (jnp.einsum('ngcp,gpd->ngcd', ca_re[:L], bb_re, precision=HI)
                      - jnp.einsum('ngcp,gpd->ngcd', ca_im[:L], bb_im, precision=HI))
        ab_re = pr[:, :, :, None] * bb_re[None] - pi[:, :, :, None] * bb_im[None]
        ab_im = pr[:, :, :, None] * bb_im[None] + pi[:, :, :, None] * bb_re[None]
        if name == "f":
            sel = jnp.arange(L - 1, -1, -1)
            vsel = jnp.arange(1, L + 1)
        else:
            sel = jnp.arange(0, L)
            vsel = jnp.arange(L, 0, -1)
        w_re = ab_re[sel]
        w_im = ab_im[sel]
        w = jnp.stack([w_re, w_im], axis=0)
        w = w.reshape(2, L, nt, 8, P, C)
        wmat = jnp.einsum('rljgpc,gh->jlgcrhp', w, eye).reshape(nt, L * 8 * C, 2 * 8 * P)
        v = jnp.stack([ca_re[vsel], -ca_im[vsel]], axis=0)
        v = v.reshape(2, L, nt, 8, C, P)
        vmat = jnp.einsum('rljgcp,gh->jrgplhc', v, eye).reshape(nt, 2 * 8 * P, L * 8 * C)
        def lanes(n):
            qr, qi = power(jnp.asarray(n))
            return jnp.concatenate([qr.reshape(-1, nt, 8 * P), qi.reshape(-1, nt, 8 * P)], axis=-1)
        rows = jnp.concatenate([
            lanes([8.0, 64.0, 512.0, 1024.0, 2048.0]),
            lanes([64.0 * (k + 1) for k in range(8)]),
            lanes([8.0 * k for k in range(8)]),
            jnp.zeros((PW_ROWS - 21, nt, 2 * 8 * P), F32),
        ], axis=0)
        outs[name] = (wmat.astype(BF16), vmat.astype(BF16), jnp.transpose(rows, (1, 0, 2)))
    kf, kb = kern["f"], kern["b"]
    li = jnp.arange(L)
    diff = li[None, :] - li[:, None]
    kfull = jnp.where((diff > 0)[:, :, None, None, None], kf[jnp.clip(diff, 0, L - 1)],
                      jnp.where((diff < 0)[:, :, None, None, None], kb[jnp.clip(-diff, 0, L - 1)],
                                (kf[0] + kb[0])[None, None]))
    kfull = kfull.reshape(L, L, nt, 8, C, C)
    tmat = jnp.einsum('lmjgcd,gh->jlgdmhc', kfull, eye).reshape(nt, L * 8 * C, L * 8 * C)
    wf, vf, pwf = outs["f"]
    wb, vb, pwb = outs["b"]
    return tmat.astype(BF16), wf, wb, vf, vb, pwf, pwb


def _glu_mm_body(a_ref, w1_ref, w2_ref, g_ref, o_ref):
    a = a_ref[...].astype(BF16)
    z1 = jnp.dot(a, w1_ref[...], preferred_element_type=F32)
    z2 = jnp.dot(a, w2_ref[...], preferred_element_type=F32)
    o_ref[...] = (z1 * jax.nn.sigmoid(z2) * _silu(g_ref[...])).astype(o_ref.dtype)


def _glu_matmul(a, w, h2, tm=512, tn=512):
    t, k = a.shape
    n = w.shape[1] // 2
    nj = n // tn
    return pl.pallas_call(
        _glu_mm_body,
        grid=(t // tm, nj),
        in_specs=[pl.BlockSpec((tm, k), lambda i, j: (i, 0)),
                  pl.BlockSpec((k, tn), lambda i, j: (0, j)),
                  pl.BlockSpec((k, tn), lambda i, j: (0, nj + j)),
                  pl.BlockSpec((tm, tn), lambda i, j: (i, O_GATE // tn + j))],
        out_specs=pl.BlockSpec((tm, tn), lambda i, j: (i, j)),
        out_shape=jax.ShapeDtypeStruct((t, n), BF16),
        compiler_params=_cparams(("parallel", "parallel")),
        name="s5_glu",
    )(a, w, w, h2)


def _swap_halves(w, width):
    k, n = w.shape
    w = w.reshape(k, n // width, 2, width // 2)
    return w[:, :, ::-1, :].reshape(k, n)


def _even_in_weight(w_in):
    k = w_in.shape[0]
    conv_in, conv_gate, c_q, c_kv, k_rope, mla_gate, mem_q, mem_gate = jnp.split(
        w_in, [2048, 3072, 3840, 4096, 4160, 5184, 5696], axis=1)
    z = jnp.zeros((k, QK_ROPE), w_in.dtype)
    krs = _swap_halves(k_rope, QK_ROPE)
    slots = [k_rope, z, z, k_rope, krs, z, z, krs]
    return jnp.concatenate([conv_in, conv_gate, c_q, c_kv, mla_gate, mem_q, mem_gate] + slots,
                           axis=1).astype(BF16)


def _uq_weight(w_uq):
    k = w_uq.shape[0]
    w = w_uq.reshape(k, MLA_HEADS, QK_NOPE + QK_ROPE)
    nope = w[:, :, :QK_NOPE].reshape(k, MLA_HEADS * QK_NOPE)
    rope = w[:, :, QK_NOPE:].reshape(k, MLA_HEADS * QK_ROPE)
    return jnp.concatenate([nope, rope, _swap_halves(rope, QK_ROPE)], axis=1).astype(BF16)


def _ukv_weight(w_ukv):
    k = w_ukv.shape[0]
    w = w_ukv.reshape(k, MLA_HEADS, QK_NOPE + V_DIM)
    kn = w[:, :, :QK_NOPE].reshape(k, MLA_HEADS * QK_NOPE)
    vv = w[:, :, QK_NOPE:].reshape(k, MLA_HEADS * V_DIM)
    return jnp.concatenate([kn, vv], axis=1).astype(BF16)


def _rope_tables(positions):
    inv_freq = ROPE_THETA ** (-jnp.arange(0, QK_ROPE, 2, dtype=F32) / QK_ROPE)
    ang = positions.astype(F32)[..., None] * inv_freq
    c, s = jnp.cos(ang), jnp.sin(ang)
    return jnp.concatenate([c, c, c, c], axis=-1), jnp.concatenate([-s, s, -s, s], axis=-1)


def _row(v):
    return v.reshape(1, -1)


def _even_layer(x2, mem2, cos128, sin128, bsz, seq, w_in, conv_w, conv_b, conv_ln_g, conv_ln_b,
                q_norm, w_uq, kv_norm, w_ukv, w_mem_kv, w_out, ln_g, ln_b):
    h = _matmul(x2, _even_in_weight(w_in), F32, 512, 512, "even_in_proj")
    a_out = _conv_branch(h, conv_w.reshape(CONV_WIDTH, CONV_DIM), _row(conv_b), _row(conv_ln_g),
                         _row(conv_ln_b), seq)
    q, k, v = _mla_proj(h, cos128, sin128, _row(q_norm), _row(kv_norm), _uq_weight(w_uq),
                        _ukv_weight(w_ukv), bsz, seq)
    b_out = _mla_attention(q, k, v, h)
    memkv = _matmul(mem2, w_mem_kv.astype(BF16), BF16, 512, 512, "even_mem_kv")
    memkv = memkv.reshape(bsz, -1, 2 * MEM_DIM)
    m_out = _mem_attention(h, memkv, E_MEM_Q, E_MEM_GATE, seq)
    wo = w_out.astype(BF16)
    return _out_ln([a_out, b_out, m_out], [wo[:1024], wo[1024:2048], wo[2048:]], x2,
                   _row(ln_g), _row(ln_b), name="even_out_ln")


def _odd_layer(x2, mem2, bsz, seq, w_in, s5_fwd, s5_bwd, s5_d, w_glu, w_mem_kv, w_out, ln_g, ln_b):
    h2 = _matmul(x2, w_in.astype(BF16), F32, 512, 512, "odd_in_proj")
    mats = _s5_mats(s5_fwd, s5_bwd)
    g = _s5_branch(h2, mats, _row(s5_d), bsz, seq)
    c_out = _glu_matmul(g, w_glu.astype(BF16), h2)
    memkv = _matmul(mem2, w_mem_kv.astype(BF16), BF16, 512, 512, "odd_mem_kv")
    memkv = memkv.reshape(bsz, -1, 2 * MEM_DIM)
    m_out = _mem_attention(h2, memkv, O_MEM_Q, O_MEM_GATE, seq)
    wo = w_out.astype(BF16)
    return _out_ln([c_out, m_out], [wo[:1024], wo[1024:]], x2, _row(ln_g), _row(ln_b),
                   name="odd_out_ln")


def kernel(x, mem, positions, e_w_in, e_conv_w, e_conv_b, e_conv_ln_g, e_conv_ln_b, e_q_norm, e_w_uq, e_kv_norm, e_w_ukv, e_mem_kv, e_w_out, e_ln_g, e_ln_b, o_w_in, o_a_re_f, o_a_im_f, o_log_dt_f, o_b_re_f, o_b_im_f, o_c_re_f, o_c_im_f, o_a_re_b, o_a_im_b, o_log_dt_b, o_b_re_b, o_b_im_b, o_c_re_b, o_c_im_b, o_d, o_w_glu, o_mem_kv, o_w_out, o_ln_g, o_ln_b):
    bsz, seq, d = x.shape
    cos128, sin128 = _rope_tables(positions)
    x2 = x.reshape(bsz * seq, d)
    mem2 = mem.reshape(-1, d)
    h = _even_layer(x2, mem2, cos128, sin128, bsz, seq, e_w_in[0], e_conv_w[0], e_conv_b[0],
                    e_conv_ln_g[0], e_conv_ln_b[0], e_q_norm[0], e_w_uq[0], e_kv_norm[0],
                    e_w_ukv[0], e_mem_kv[0], e_w_out[0], e_ln_g[0], e_ln_b[0])
    s5_fwd = (o_a_re_f[0], o_a_im_f[0], o_log_dt_f[0], o_b_re_f[0], o_b_im_f[0], o_c_re_f[0], o_c_im_f[0])
    s5_bwd = (o_a_re_b[0], o_a_im_b[0], o_log_dt_b[0], o_b_re_b[0], o_b_im_b[0], o_c_re_b[0], o_c_im_b[0])
    h = _odd_layer(h, mem2, bsz, seq, o_w_in[0], s5_fwd, s5_bwd, o_d[0], o_w_glu[0], o_mem_kv[0],
                   o_w_out[0], o_ln_g[0], o_ln_b[0])
    return h.reshape(bsz, seq, d)
```

```python
import functools
import math

import jax
import jax.numpy as jnp
from jax import lax
from jax.experimental import pallas as pl
from jax.experimental.pallas import tpu as pltpu

F32 = jnp.float32
BF16 = jnp.bfloat16
HI = lax.Precision.HIGHEST

D_MODEL = 1024
CONV_DIM = 1024
CONV_WIDTH = 31
CONV_PAD = 15
MLA_HEADS = 8
QK_NOPE = 128
QK_ROPE = 64
V_DIM = 128
Q_LORA = 768
KV_LORA = 256
ROPE_THETA = 10000.0
MEM_HEADS = 4
MEM_HEAD_DIM = 128
MEM_DIM = 512
S5_DIM = 1024
S5_GROUP = 16
S5_GROUPS = 64
S5_STATE = 64
LN_EPS = 1e-5
RMS_EPS = 1e-6
DEPTH = 2
ALPHA = (2 * DEPTH) ** 0.25

LANE = 128
SUBLANE = 8
VMEM_LIMIT = 56 * 1024 * 1024

E_CONV_IN = 0
E_CONV_GATE = 2048
E_CQ = 3072
E_CKV = 3840
E_MLA_GATE = 4096
E_MEM_Q = 5120
E_MEM_GATE = 5632
E_KROPE = 6144
E_NPAD = 6656
O_SPLIT = 1024
O_GATE = 0
O_MEM_Q = 1024
O_MEM_GATE = 1536

S5_L = 8


def _cparams(sem):
    return pltpu.CompilerParams(dimension_semantics=sem, vmem_limit_bytes=VMEM_LIMIT)


def _silu(x):
    return x * jax.nn.sigmoid(x)


def _mm_body(a_ref, w_ref, o_ref, a16):
    @pl.when(pl.program_id(1) == 0)
    def _():
        a16[...] = a_ref[...].astype(BF16)

    o_ref[...] = jnp.dot(a16[...], w_ref[...], preferred_element_type=F32).astype(o_ref.dtype)


def _matmul(a, w, out_dtype, tm, tn, name):
    m, k = a.shape
    n = w.shape[1]
    tm = min(tm, m)
    assert m % tm == 0 and n % tn == 0
    return pl.pallas_call(
        _mm_body,
        grid=(m // tm, n // tn),
        in_specs=[pl.BlockSpec((tm, k), lambda i, j: (i, 0)),
                  pl.BlockSpec((k, tn), lambda i, j: (0, j))],
        out_specs=pl.BlockSpec((tm, tn), lambda i, j: (i, j)),
        out_shape=jax.ShapeDtypeStruct((m, n), out_dtype),
        scratch_shapes=[pltpu.VMEM((tm, k), BF16)],
        compiler_params=_cparams(("parallel", "arbitrary")),
        name=name,
    )(a, w)


CONV_HALO = 16
CONV_RC = 32
CONV_COPY_ROWS = 40


def _conv_body(m1, m2, p1, p2, n1, n2, gate, cw, cb, lg, lb, o_ref, slab, *, ts, tiles_per_seq):
    i = pl.program_id(0)
    first = (i % tiles_per_seq) == 0
    last = (i % tiles_per_seq) == tiles_per_seq - 1

    def glu(a, b):
        return a[...].astype(F32) * jax.nn.sigmoid(b[...].astype(F32))

    slab[0, pl.ds(CONV_HALO, ts), :] = glu(m1, m2)
    slab[0, pl.ds(0, CONV_HALO), :] = jnp.where(first, 0.0, glu(p1, p2))
    slab[0, pl.ds(CONV_HALO + ts, CONV_HALO), :] = jnp.where(last, 0.0, glu(n1, n2))

    span = ts + 2 * CONV_HALO - SUBLANE
    for r in range(1, SUBLANE):
        for c0 in range(0, span, CONV_COPY_ROWS):
            n = min(CONV_COPY_ROWS, span - c0)
            slab[r, pl.ds(c0, n), :] = slab[0, pl.ds(c0 + r, n), :]

    off = CONV_HALO - CONV_PAD
    for rc in range(ts // CONV_RC):
        base = rc * CONV_RC
        acc = jnp.zeros((CONV_RC, CONV_DIM), F32)
        for k in range(CONV_WIDTH):
            r = (off + k) % SUBLANE
            acc = acc + slab[r, pl.ds(base + off + k - r, CONV_RC), :] * cw[pl.ds(k, 1), :]
        dw = acc + cb[...]
        mu = jnp.mean(dw, axis=-1, keepdims=True)
        xc = dw - mu
        var = jnp.mean(xc * xc, axis=-1, keepdims=True)
        y = xc * lax.rsqrt(var + LN_EPS) * lg[...] + lb[...]
        res = _silu(y) * _silu(gate[pl.ds(base, CONV_RC), :].astype(F32))
        o_ref[pl.ds(base, CONV_RC), :] = res.astype(o_ref.dtype)


def _conv_branch(h, conv_w, conv_b, ln_g, ln_b, seq, ts=256):
    t = h.shape[0]
    nb = ts // CONV_HALO
    last_blk = t // CONV_HALO - 1
    c1, c2, cg = E_CONV_IN // CONV_DIM, E_CONV_IN // CONV_DIM + 1, E_CONV_GATE // CONV_DIM
    body = functools.partial(_conv_body, ts=ts, tiles_per_seq=seq // ts)
    vec = pl.BlockSpec((1, CONV_DIM), lambda i: (0, 0))
    return pl.pallas_call(
        body,
        grid=(t // ts,),
        in_specs=[
            pl.BlockSpec((ts, CONV_DIM), lambda i: (i, c1)),
            pl.BlockSpec((ts, CONV_DIM), lambda i: (i, c2)),
            pl.BlockSpec((CONV_HALO, CONV_DIM), lambda i: (jnp.maximum(i * nb - 1, 0), c1)),
            pl.BlockSpec((CONV_HALO, CONV_DIM), lambda i: (jnp.maximum(i * nb - 1, 0), c2)),
            pl.BlockSpec((CONV_HALO, CONV_DIM), lambda i: (jnp.minimum((i + 1) * nb, last_blk), c1)),
            pl.BlockSpec((CONV_HALO, CONV_DIM), lambda i: (jnp.minimum((i + 1) * nb, last_blk), c2)),
            pl.BlockSpec((ts, CONV_DIM), lambda i: (i, cg)),
            pl.BlockSpec((CONV_WIDTH, CONV_DIM), lambda i: (0, 0)),
            vec, vec, vec,
        ],
        out_specs=pl.BlockSpec((ts, CONV_DIM), lambda i: (i, 0)),
        out_shape=jax.ShapeDtypeStruct((t, CONV_DIM), BF16),
        scratch_shapes=[pltpu.VMEM((SUBLANE, ts + 2 * CONV_HALO, CONV_DIM), F32)],
        compiler_params=_cparams(("parallel",)),
        name="conv_branch",
    )(h, h, h, h, h, h, h, conv_w, conv_b, ln_g, ln_b)


def _rms(x, g):
    ms = jnp.mean(x * x, axis=-1, keepdims=True)
    return x * lax.rsqrt(ms + RMS_EPS) * g


def _mla_proj_body(cq, ckv, kr, cos, sin, qn_g, kvn_g, wq, wkv, q_ref, k_ref, v_ref, *, scale):
    c = cos[0]
    s = sin[0]
    nq = _rms(cq[...].astype(F32), qn_g[...]).astype(BF16)
    qf = jnp.dot(nq, wq[...], preferred_element_type=F32)
    nh = MLA_HEADS * QK_NOPE
    nr = MLA_HEADS * QK_ROPE
    c4 = jnp.concatenate([c] * (nr // LANE), axis=-1)
    s4 = jnp.concatenate([s] * (nr // LANE), axis=-1)
    qr = qf[:, nh:nh + nr] * c4 + qf[:, nh + nr:nh + 2 * nr] * s4
    nkv = _rms(ckv[...].astype(F32), kvn_g[...]).astype(BF16)
    kvf = jnp.dot(nkv, wkv[...], preferred_element_type=F32)
    krf = kr[...].astype(F32)
    kr_even = krf[:, 0:LANE] * c + krf[:, 2 * LANE:3 * LANE] * s
    kr_odd = krf[:, LANE:2 * LANE] * c + krf[:, 3 * LANE:4 * LANE] * s
    for h in range(MLA_HEADS):
        q_ref[0, h, :, 0:QK_NOPE] = (qf[:, h * QK_NOPE:(h + 1) * QK_NOPE] * scale).astype(BF16)
        p = h // 2
        q_ref[0, h, :, QK_NOPE:2 * QK_NOPE] = (qr[:, p * LANE:(p + 1) * LANE] * scale).astype(BF16)
        k_ref[0, h, :, 0:QK_NOPE] = kvf[:, h * QK_NOPE:(h + 1) * QK_NOPE].astype(BF16)
        k_ref[0, h, :, QK_NOPE:2 * QK_NOPE] = (kr_even if h % 2 == 0 else kr_odd).astype(BF16)
        v_ref[0, h, :, :] = kvf[:, nh + h * V_DIM:nh + (h + 1) * V_DIM].astype(BF16)


def _mla_proj(h, cos128, sin128, q_norm, kv_norm, wq, wkv, bsz, seq, tm=512):
    nt = seq // tm
    scale = (QK_NOPE + QK_ROPE) ** -0.5 * math.log2(math.e)
    body = functools.partial(_mla_proj_body, scale=scale)
    hd = 2 * QK_NOPE
    return pl.pallas_call(
        body,
        grid=(bsz, nt),
        in_specs=[
            pl.BlockSpec((tm, Q_LORA), lambda b, i: (b * nt + i, E_CQ // Q_LORA)),
            pl.BlockSpec((tm, KV_LORA), lambda b, i: (b * nt + i, E_CKV // KV_LORA)),
            pl.BlockSpec((tm, 4 * LANE), lambda b, i: (b * nt + i, E_KROPE // (4 * LANE))),
            pl.BlockSpec((1, tm, LANE), lambda b, i: (b, i, 0)),
            pl.BlockSpec((1, tm, LANE), lambda b, i: (b, i, 0)),
            pl.BlockSpec((1, Q_LORA), lambda b, i: (0, 0)),
            pl.BlockSpec((1, KV_LORA), lambda b, i: (0, 0)),
            pl.BlockSpec(wq.shape, lambda b, i: (0, 0)),
            pl.BlockSpec(wkv.shape, lambda b, i: (0, 0)),
        ],
        out_specs=[
            pl.BlockSpec((1, MLA_HEADS, tm, hd), lambda b, i: (b, 0, i, 0)),
            pl.BlockSpec((1, MLA_HEADS, tm, hd), lambda b, i: (b, 0, i, 0)),
            pl.BlockSpec((1, MLA_HEADS, tm, V_DIM), lambda b, i: (b, 0, i, 0)),
        ],
        out_shape=[
            jax.ShapeDtypeStruct((bsz, MLA_HEADS, seq, hd), BF16),
            jax.ShapeDtypeStruct((bsz, MLA_HEADS, seq, hd), BF16),
            jax.ShapeDtypeStruct((bsz, MLA_HEADS, seq, V_DIM), BF16),
        ],
        compiler_params=_cparams(("parallel", "parallel")),
        name="mla_proj",
    )(h, h, h, cos128, sin128, q_norm, kv_norm, wq, wkv)


def _flash_body(q_ref, k_ref, v_ref, g_ref, o_ref, *, tk, nk, unroll):
    q = q_ref[0, 0]
    tq = q.shape[0]

    def step(c, carry):
        m, l, acc = carry
        start = pl.multiple_of(c * tk, tk)
        ks = k_ref[0, 0, pl.ds(start, tk), :]
        vs = v_ref[0, 0, pl.ds(start, tk), :]
        s = lax.dot_general(q, ks, (((1,), (1,)), ((), ())), preferred_element_type=F32)
        m_new = jnp.maximum(m, jnp.max(s, axis=-1, keepdims=True))
        alpha = jnp.exp2(m - m_new)
        p = jnp.exp2(s - m_new)
        l = alpha * l + jnp.sum(p, axis=-1, keepdims=True)
        acc = alpha * acc + jnp.dot(p.astype(BF16), vs, preferred_element_type=F32)
        return m_new, l, acc

    m0 = jnp.full((tq, 1), -jnp.inf, F32)
    l0 = jnp.zeros((tq, 1), F32)
    a0 = jnp.zeros((tq, V_DIM), F32)
    m, l, acc = lax.fori_loop(0, nk, step, (m0, l0, a0), unroll=unroll)
    o_ref[...] = (acc / l * _silu(g_ref[...].astype(F32))).astype(o_ref.dtype)


def _mla_attention(q, k, v, h, tq=512, tk=512, unroll=2):
    bsz, heads, seq, hd = q.shape
    nq = seq // tq
    body = functools.partial(_flash_body, tk=tk, nk=seq // tk, unroll=unroll)
    gcol = E_MLA_GATE // V_DIM
    return pl.pallas_call(
        body,
        grid=(bsz, heads, nq),
        in_specs=[
            pl.BlockSpec((1, 1, tq, hd), lambda b, hh, i: (b, hh, i, 0)),
            pl.BlockSpec((1, 1, seq, hd), lambda b, hh, i: (b, hh, 0, 0)),
            pl.BlockSpec((1, 1, seq, V_DIM), lambda b, hh, i: (b, hh, 0, 0)),
            pl.BlockSpec((tq, V_DIM), lambda b, hh, i: (b * nq + i, gcol + hh)),
        ],
        out_specs=pl.BlockSpec((tq, V_DIM), lambda b, hh, i: (b * nq + i, hh)),
        out_shape=jax.ShapeDtypeStruct((bsz * seq, heads * V_DIM), BF16),
        compiler_params=_cparams(("parallel", "parallel", "arbitrary")),
        name="mla_attention",
    )(q, k, v, h)


def _mem_attn_body(q_ref, g_ref, kv_ref, o_ref, *, scale):
    for hh in range(MEM_HEADS):
        lo, hi = hh * MEM_HEAD_DIM, (hh + 1) * MEM_HEAD_DIM
        kk = kv_ref[0, :, lo:hi]
        vv = kv_ref[0, :, MEM_DIM + lo:MEM_DIM + hi]
        s = lax.dot_general(q_ref[:, lo:hi], kk, (((1,), (1,)), ((), ())),
                            preferred_element_type=F32) * scale
        m = jnp.max(s, axis=-1, keepdims=True)
        p = jnp.exp(s - m)
        l = jnp.sum(p, axis=-1, keepdims=True)
        o = jnp.dot(p.astype(BF16), vv, preferred_element_type=F32) / l
        o_ref[:, lo:hi] = (o * _silu(g_ref[:, lo:hi].astype(F32))).astype(o_ref.dtype)


def _mem_attention(h, memkv, q_off, g_off, seq, tm=512):
    t = h.shape[0]
    nt = seq // tm
    mlen = memkv.shape[1]
    body = functools.partial(_mem_attn_body, scale=MEM_HEAD_DIM ** -0.5)
    return pl.pallas_call(
        body,
        grid=(t // tm,),
        in_specs=[
            pl.BlockSpec((tm, MEM_DIM), lambda i: (i, q_off // MEM_DIM)),
            pl.BlockSpec((tm, MEM_DIM), lambda i: (i, g_off // MEM_DIM)),
            pl.BlockSpec((1, mlen, 2 * MEM_DIM), lambda i: (i // nt, 0, 0)),
        ],
        out_specs=pl.BlockSpec((tm, MEM_DIM), lambda i: (i, 0)),
        out_shape=jax.ShapeDtypeStruct((t, MEM_DIM), BF16),
        compiler_params=_cparams(("parallel",)),
        name="mem_attention",
    )(h, h, memkv)


def _out_ln_body(*refs, nparts):
    parts = refs[:nparts]
    ws = refs[nparts:2 * nparts]
    x_ref, g_ref, b_ref, o_ref = refs[2 * nparts:]
    y = jnp.dot(parts[0][...], ws[0][...], preferred_element_type=F32)
    for p, w in zip(parts[1:], ws[1:]):
        y = y + jnp.dot(p[...], w[...], preferred_element_type=F32)
    z = ALPHA * x_ref[...] + y
    mu = jnp.mean(z, axis=-1, keepdims=True)
    zc = z - mu
    var = jnp.mean(zc * zc, axis=-1, keepdims=True)
    o_ref[...] = zc * lax.rsqrt(var + LN_EPS) * g_ref[...] + b_ref[...]


def _out_ln(parts, ws, x, g, b, tm=512, name="out_ln"):
    t = x.shape[0]
    n = len(parts)
    body = functools.partial(_out_ln_body, nparts=n)
    in_specs = [pl.BlockSpec((tm, p.shape[1]), lambda i: (i, 0)) for p in parts]
    in_specs += [pl.BlockSpec(w.shape, lambda i: (0, 0)) for w in ws]
    in_specs += [pl.BlockSpec((tm, D_MODEL), lambda i: (i, 0)),
                 pl.BlockSpec((1, D_MODEL), lambda i: (0, 0)),
                 pl.BlockSpec((1, D_MODEL), lambda i: (0, 0))]
    return pl.pallas_call(
        body,
        grid=(t // tm,),
        in_specs=in_specs,
        out_specs=pl.BlockSpec((tm, D_MODEL), lambda i: (i, 0)),
        out_shape=jax.ShapeDtypeStruct((t, D_MODEL), F32),
        compiler_params=_cparams(("parallel",)),
        name=name,
    )(*parts, *ws, x, g, b)


S5_ROWS_PER_B = 8
PW_A8, PW_A64, PW_A512, PW_A1024, PW_A2048 = 0, 1, 2, 3, 4
PW_A64K = 5
PW_A8K = 13
PW_ROWS = 24


def _cmul(pr, pi, xr, xi):
    return pr * xr - pi * xi, pr * xi + pi * xr


def _s5_scan(s_ref, x_ref, pw_ref, rev, bsz):
    half = 512
    n0 = 8 * bsz * S5_ROWS_PER_B
    n1 = bsz * S5_ROWS_PER_B

    def pw(row):
        v = pw_ref[0, pl.ds(row, 1), :]
        return v[:, :half], v[:, half:]

    def blk0(k):
        return (7 - k if rev else k) * n0

    def blk1(k):
        return (7 - k if rev else k) * n1

    def ld(ref, start, n):
        v = ref[pl.ds(start, n), :]
        return v[:, :half], v[:, half:]

    def st(ref, start, n, re, im):
        ref[pl.ds(start, n), 0:half] = re
        ref[pl.ds(start, n), half:2 * half] = im

    a8r, a8i = pw(PW_A8)
    for k in range(1, 8):
        pr_, pi_ = ld(s_ref, blk0(k - 1), n0)
        cr, ci = ld(s_ref, blk0(k), n0)
        mr, mi = _cmul(a8r, a8i, pr_, pi_)
        st(s_ref, blk0(k), n0, cr + mr, ci + mi)
    g0 = blk0(7)
    a64r, a64i = pw(PW_A64)
    for k in range(1, 8):
        pr_, pi_ = ld(s_ref, g0 + blk1(k - 1), n1)
        cr, ci = ld(s_ref, g0 + blk1(k), n1)
        mr, mi = _cmul(a64r, a64i, pr_, pi_)
        st(s_ref, g0 + blk1(k), n1, cr + mr, ci + mi)
    hr, hi = ld(s_ref, g0 + blk1(7), n1)
    c2 = lax.broadcasted_iota(jnp.int32, (n1, half), 0) % S5_ROWS_PER_B
    if rev:
        c2 = (S5_ROWS_PER_B - 1) - c2
    for d, row in ((1, PW_A512), (2, PW_A1024), (4, PW_A2048)):
        ar, ai = pw(row)
        sh = (n1 - d) if rev else d
        sr = pltpu.roll(hr, sh, 0)
        si = pltpu.roll(hi, sh, 0)
        mr, mi = _cmul(ar, ai, sr, si)
        keep = c2 >= d
        hr = hr + jnp.where(keep, mr, 0.0)
        hi = hi + jnp.where(keep, mi, 0.0)
    sh1 = (n1 - 1) if rev else 1
    p3r = jnp.where(c2 >= 1, pltpu.roll(hr, sh1, 0), 0.0)
    p3i = jnp.where(c2 >= 1, pltpu.roll(hi, sh1, 0), 0.0)
    for k1 in range(8):
        if k1 == 0:
            er, ei = p3r, p3i
        else:
            ar, ai = pw(PW_A64K + k1 - 1)
            mr, mi = _cmul(ar, ai, p3r, p3i)
            qr, qi = ld(s_ref, g0 + blk1(k1 - 1), n1)
            er, ei = qr + mr, qi + mi
        for k0 in range(8):
            dst = blk0(k0) + blk1(k1)
            if k0 == 0:
                st(x_ref, dst, n1, er, ei)
            else:
                ar, ai = pw(PW_A8K + k0)
                mr, mi = _cmul(ar, ai, er, ei)
                qr, qi = ld(s_ref, blk0(k0 - 1) + blk1(k1), n1)
                st(x_ref, dst, n1, qr + mr, qi + mi)


def _expand_blockdiag(c_ref, m_ref, *, row_b_log2, d_log2):
    comp = c_ref[0]
    rows = comp.shape[0]
    rowg = lax.shift_right_logical(lax.broadcasted_iota(jnp.int32, (rows, LANE), 0), row_b_log2) & 7
    coln = lax.broadcasted_iota(jnp.int32, (rows, LANE), 1)
    kk = lax.broadcasted_iota(jnp.int32, (LANE, LANE), 0)
    nn = lax.broadcasted_iota(jnp.int32, (LANE, LANE), 1)
    dmask = (1 << d_log2) - 1
    for tt in range(8):
        col = tt * LANE + nn
        src = lax.shift_left(lax.shift_right_logical(col, d_log2 + 3), d_log2) + (col & dmask)
        sel = jnp.where(kk == src, 1.0, 0.0).astype(BF16)
        piece = jnp.dot(comp, sel, preferred_element_type=F32)
        colg = lax.shift_right_logical(tt * LANE + coln, d_log2) & 7
        m_ref[:, tt * LANE:(tt + 1) * LANE] = jnp.where(rowg == colg, piece, 0.0).astype(BF16)


def _s5_body(u_ref, ct_ref, cwf_ref, cwb_ref, cvf_ref, cvb_ref, pwf_ref, pwb_ref, d_ref, o_ref,
             tw, wf, wb, vf, vb, xcat, sbuf, xf, xb, *, seq):
    n1 = S5_ROWS_PER_B
    tok_stride = seq // S5_ROWS_PER_B

    @pl.when(pl.program_id(1) == 0)
    def _():
        _expand_blockdiag(ct_ref, tw, row_b_log2=4, d_log2=4)
        _expand_blockdiag(cwf_ref, wf, row_b_log2=4, d_log2=6)
        _expand_blockdiag(cwb_ref, wb, row_b_log2=4, d_log2=6)
        _expand_blockdiag(cvf_ref, vf, row_b_log2=6, d_log2=4)
        _expand_blockdiag(cvb_ref, vb, row_b_log2=6, d_log2=4)

    def pieces():
        for c0 in range(8):
            for c1 in range(8):
                for l in range(S5_L):
                    yield (c0 * 8 * n1 + c1 * n1, l * LANE,
                           pl.ds(c1 * 64 + c0 * 8 + l, n1, stride=tok_stride))

    for r0, l0, tok in pieces():
        xcat[pl.ds(r0, n1), l0:l0 + LANE] = u_ref[tok, :]
    xb16 = xcat[...].astype(BF16)
    sbuf[...] = jnp.dot(xb16, wf[...], preferred_element_type=F32)
    _s5_scan(sbuf, xf, pwf_ref, False, 1)
    sbuf[...] = jnp.dot(xb16, wb[...], preferred_element_type=F32)
    _s5_scan(sbuf, xb, pwb_ref, True, 1)
    y = jnp.dot(xb16, tw[...], preferred_element_type=F32)
    y = y + jnp.dot(xf[...].astype(BF16), vf[...], preferred_element_type=F32)
    y = y + jnp.dot(xb[...].astype(BF16), vb[...], preferred_element_type=F32)
    dd = jnp.concatenate([d_ref[...]] * S5_L, axis=-1)
    sbuf[...] = jax.nn.gelu(y + dd * xcat[...])
    for r0, l0, tok in pieces():
        o_ref[tok, :] = sbuf[pl.ds(r0, n1), l0:l0 + LANE].astype(o_ref.dtype)


def _s5_branch(u, mats, d, bsz, seq):
    t = u.shape[0]
    ct, cwf, cwb, cvf, cvb, pwf, pwb = mats
    ntile = S5_DIM // LANE
    rows = seq // S5_L
    wide = S5_L * LANE
    body = functools.partial(_s5_body, seq=seq)
    comp = pl.BlockSpec((1, wide, LANE), lambda j, b: (j, 0, 0))
    pws = pl.BlockSpec((1, PW_ROWS, wide), lambda j, b: (j, 0, 0))
    return pl.pallas_call(
        body,
        grid=(ntile, bsz),
        in_specs=[pl.BlockSpec((seq, LANE), lambda j, b: (b, j)),
                  comp, comp, comp, comp, comp, pws, pws,
                  pl.BlockSpec((1, LANE), lambda j, b: (0, j))],
        out_specs=pl.BlockSpec((seq, LANE), lambda j, b: (b, j)),
        out_shape=jax.ShapeDtypeStruct((t, S5_DIM), F32),
        scratch_shapes=[pltpu.VMEM((wide, wide), BF16)] * 5 + [pltpu.VMEM((rows, wide), F32)] * 4,
        compiler_params=_cparams(("parallel", "arbitrary")),
        name="s5_branch",
    )(u, ct, cwf, cwb, cvf, cvb, pwf, pwb, d)


def _s5_dir_params(a_re, a_im, log_dt, b_re, b_im, c_re, c_im):
    lam_re = jnp.minimum(a_re, -1e-4)
    lam_im = a_im
    dt = jnp.exp(log_dt)[:, None]
    mag = jnp.exp(lam_re * dt)
    lb_re = mag * jnp.cos(lam_im * dt)
    lb_im = mag * jnp.sin(lam_im * dt)
    den = lam_re * lam_re + lam_im * lam_im
    nr = lb_re - 1.0
    f_re = (nr * lam_re + lb_im * lam_im) / den
    f_im = (lb_im * lam_re - nr * lam_im) / den
    bb_re = f_re[..., None] * b_re - f_im[..., None] * b_im
    bb_im = f_re[..., None] * b_im + f_im[..., None] * b_re

    def power(n):
        n = jnp.asarray(n, F32)
        shape = n.shape + (1, 1)
        n = n.reshape(shape)
        m = jnp.exp(n * (lam_re * dt))
        ang = n * (lam_im * dt)
        return m * jnp.cos(ang), m * jnp.sin(ang)

    return bb_re, bb_im, c_re, c_im, power


def _s5_mats(fwd, bwd):
    G, P, C, L = S5_GROUPS, S5_STATE, S5_GROUP, S5_L
    nt = G // 8
    outs = {}
    kern = {}
    for name, prm in (("f", fwd), ("b", bwd)):
        bb_re, bb_im, c_re, c_im, power = _s5_dir_params(*prm)
        pr, pi = power(jnp.arange(L + 1))
        ca_re = c_re[None] * pr[:, :, None, :] - c_im[None] * pi[:, :, None, :]
        ca_im = c_re[None] * pi[:, :, None, :] + c_im[None] * pr[:, :, None, :]
        kern[name] = (jnp.einsum('ngcp,gpd->ngcd', ca_re[:L], bb_re, precision=HI)
                      - jnp.einsum('ngcp,gpd->ngcd', ca_im[:L], bb_im, precision=HI))
        ab_re = pr[:, :, :, None] * bb_re[None] - pi[:, :, :, None] * bb_im[None]
        ab_im = pr[:, :, :, None] * bb_im[None] + pi[:, :, :, None] * bb_re[None]
        if name == "f":
            sel = jnp.arange(L - 1, -1, -1)
            vsel = jnp.arange(1, L + 1)
        else:
            sel = jnp.arange(0, L)
            vsel = jnp.arange(L, 0, -1)
        w_re = ab_re[sel]
        w_im = ab_im[sel]
        w = jnp.stack([w_re, w_im], axis=0)
        w = w.reshape(2, L, nt, 8, P, C)
        wmat = jnp.transpose(w, (2, 1, 3, 5, 0, 4)).reshape(nt, L * 8 * C, 2 * P)
        v = jnp.stack([ca_re[vsel], -ca_im[vsel]], axis=0)
        v = v.reshape(2, L, nt, 8, C, P)
        vmat = jnp.transpose(v, (2, 0, 3, 5, 1, 4)).reshape(nt, 2 * 8 * P, L * C)
        def lanes(n):
            qr, qi = power(jnp.asarray(n))
            return jnp.concatenate([qr.reshape(-1, nt, 8 * P), qi.reshape(-1, nt, 8 * P)], axis=-1)
        rows = jnp.concatenate([
            lanes([8.0, 64.0, 512.0, 1024.0, 2048.0]),
            lanes([64.0 * (k + 1) for k in range(8)]),
            lanes([8.0 * k for k in range(8)]),
            jnp.zeros((PW_ROWS - 21, nt, 2 * 8 * P), F32),
        ], axis=0)
        outs[name] = (wmat.astype(BF16), vmat.astype(BF16), jnp.transpose(rows, (1, 0, 2)))
    kf, kb = kern["f"], kern["b"]
    li = jnp.arange(L)
    diff = li[None, :] - li[:, None]
    kfull = jnp.where((diff > 0)[:, :, None, None, None], kf[jnp.clip(diff, 0, L - 1)],
                      jnp.where((diff < 0)[:, :, None, None, None], kb[jnp.clip(-diff, 0, L - 1)],
                                (kf[0] + kb[0])[None, None]))
    kfull = kfull.reshape(L, L, nt, 8, C, C)
    tmat = jnp.transpose(kfull, (2, 0, 3, 5, 1, 4)).reshape(nt, L * 8 * C, L * C)
    wf, vf, pwf = outs["f"]
    wb, vb, pwb = outs["b"]
    return tmat.astype(BF16), wf, wb, vf, vb, pwf, pwb


def _glu_mm_body(a_ref, w1_ref, w2_ref, g_ref, o_ref):
    a = a_ref[...].astype(BF16)
    z1 = jnp.dot(a, w1_ref[...], preferred_element_type=F32)
    z2 = jnp.dot(a, w2_ref[...], preferred_element_type=F32)
    o_ref[...] = (z1 * jax.nn.sigmoid(z2) * _silu(g_ref[...].astype(F32))).astype(o_ref.dtype)


def _glu_matmul(a, w, h2, tm=512, tn=512):
    t, k = a.shape
    n = w.shape[1] // 2
    nj = n // tn
    return pl.pallas_call(
        _glu_mm_body,
        grid=(t // tm, nj),
        in_specs=[pl.BlockSpec((tm, k), lambda i, j: (i, 0)),
                  pl.BlockSpec((k, tn), lambda i, j: (0, j)),
                  pl.BlockSpec((k, tn), lambda i, j: (0, nj + j)),
                  pl.BlockSpec((tm, tn), lambda i, j: (i, O_GATE // tn + j))],
        out_specs=pl.BlockSpec((tm, tn), lambda i, j: (i, j)),
        out_shape=jax.ShapeDtypeStruct((t, n), BF16),
        compiler_params=_cparams(("parallel", "parallel")),
        name="s5_glu",
    )(a, w, w, h2)


def _swap_halves(w, width):
    k, n = w.shape
    w = w.reshape(k, n // width, 2, width // 2)
    return w[:, :, ::-1, :].reshape(k, n)


def _even_in_weight(w_in):
    k = w_in.shape[0]
    conv_in, conv_gate, c_q, c_kv, k_rope, mla_gate, mem_q, mem_gate = jnp.split(
        w_in, [2048, 3072, 3840, 4096, 4160, 5184, 5696], axis=1)
    z = jnp.zeros((k, QK_ROPE), w_in.dtype)
    krs = _swap_halves(k_rope, QK_ROPE)
    slots = [k_rope, z, z, k_rope, krs, z, z, krs]
    return jnp.concatenate([conv_in, conv_gate, c_q, c_kv, mla_gate, mem_q, mem_gate] + slots,
                           axis=1).astype(BF16)


def _uq_weight(w_uq):
    k = w_uq.shape[0]
    w = w_uq.reshape(k, MLA_HEADS, QK_NOPE + QK_ROPE)
    nope = w[:, :, :QK_NOPE].reshape(k, MLA_HEADS * QK_NOPE)
    rope = w[:, :, QK_NOPE:].reshape(k, MLA_HEADS * QK_ROPE)
    return jnp.concatenate([nope, rope, _swap_halves(rope, QK_ROPE)], axis=1).astype(BF16)


def _ukv_weight(w_ukv):
    k = w_ukv.shape[0]
    w = w_ukv.reshape(k, MLA_HEADS, QK_NOPE + V_DIM)
    kn = w[:, :, :QK_NOPE].reshape(k, MLA_HEADS * QK_NOPE)
    vv = w[:, :, QK_NOPE:].reshape(k, MLA_HEADS * V_DIM)
    return jnp.concatenate([kn, vv], axis=1).astype(BF16)


def _rope_tables(positions):
    inv_freq = ROPE_THETA ** (-jnp.arange(0, QK_ROPE, 2, dtype=F32) / QK_ROPE)
    ang = positions.astype(F32)[..., None] * inv_freq
    c, s = jnp.cos(ang), jnp.sin(ang)
    return jnp.concatenate([c, c, c, c], axis=-1), jnp.concatenate([-s, s, -s, s], axis=-1)


def _row(v):
    return v.reshape(1, -1)


def _even_layer(x2, mem2, cos128, sin128, bsz, seq, w_in, conv_w, conv_b, conv_ln_g, conv_ln_b,
                q_norm, w_uq, kv_norm, w_ukv, w_mem_kv, w_out, ln_g, ln_b):
    h = _matmul(x2, _even_in_weight(w_in), BF16, 2048, 512, "even_in_proj")
    a_out = _conv_branch(h, conv_w.reshape(CONV_WIDTH, CONV_DIM), _row(conv_b), _row(conv_ln_g),
                         _row(conv_ln_b), seq)
    q, k, v = _mla_proj(h, cos128, sin128, _row(q_norm), _row(kv_norm), _uq_weight(w_uq),
                        _ukv_weight(w_ukv), bsz, seq)
    b_out = _mla_attention(q, k, v, h, tq=1024, tk=1024, unroll=True)
    memkv = _matmul(mem2, w_mem_kv.astype(BF16), BF16, 512, 512, "even_mem_kv")
    memkv = memkv.reshape(bsz, -1, 2 * MEM_DIM)
    m_out = _mem_attention(h, memkv, E_MEM_Q, E_MEM_GATE, seq)
    wo = w_out.astype(BF16)
    return _out_ln([a_out, b_out, m_out], [wo[:1024], wo[1024:2048], wo[2048:]], x2,
                   _row(ln_g), _row(ln_b), name="even_out_ln")


def _odd_layer(x2, mem2, bsz, seq, w_in, s5_fwd, s5_bwd, s5_d, w_glu, w_mem_kv, w_out, ln_g, ln_b):
    wi = w_in.astype(BF16)
    u = _matmul(x2, wi[:, :O_SPLIT], F32, 2048, 512, "odd_in_proj_u")
    h2 = _matmul(x2, wi[:, O_SPLIT:], BF16, 2048, 512, "odd_in_proj")
    mats = _s5_mats(s5_fwd, s5_bwd)
    g = _s5_branch(u, mats, _row(s5_d), bsz, seq)
    c_out = _glu_matmul(g, w_glu.astype(BF16), h2)
    memkv = _matmul(mem2, w_mem_kv.astype(BF16), BF16, 512, 512, "odd_mem_kv")
    memkv = memkv.reshape(bsz, -1, 2 * MEM_DIM)
    m_out = _mem_attention(h2, memkv, O_MEM_Q, O_MEM_GATE, seq)
    wo = w_out.astype(BF16)
    return _out_ln([c_out, m_out], [wo[:1024], wo[1024:]], x2, _row(ln_g), _row(ln_b),
                   name="odd_out_ln")


def kernel(x, mem, positions, e_w_in, e_conv_w, e_conv_b, e_conv_ln_g, e_conv_ln_b, e_q_norm, e_w_uq, e_kv_norm, e_w_ukv, e_mem_kv, e_w_out, e_ln_g, e_ln_b, o_w_in, o_a_re_f, o_a_im_f, o_log_dt_f, o_b_re_f, o_b_im_f, o_c_re_f, o_c_im_f, o_a_re_b, o_a_im_b, o_log_dt_b, o_b_re_b, o_b_im_b, o_c_re_b, o_c_im_b, o_d, o_w_glu, o_mem_kv, o_w_out, o_ln_g, o_ln_b):
    bsz, seq, d = x.shape
    cos128, sin128 = _rope_tables(positions)
    x2 = x.reshape(bsz * seq, d)
    mem2 = mem.reshape(-1, d)
    h = _even_layer(x2, mem2, cos128, sin128, bsz, seq, e_w_in[0], e_conv_w[0], e_conv_b[0],
                    e_conv_ln_g[0], e_conv_ln_b[0], e_q_norm[0], e_w_uq[0], e_kv_norm[0],
                    e_w_ukv[0], e_mem_kv[0], e_w_out[0], e_ln_g[0], e_ln_b[0])
    s5_fwd = (o_a_re_f[0], o_a_im_f[0], o_log_dt_f[0], o_b_re_f[0], o_b_im_f[0], o_c_re_f[0], o_c_im_f[0])
    s5_bwd = (o_a_re_b[0], o_a_im_b[0], o_log_dt_b[0], o_b_re_b[0], o_b_im_b[0], o_c_re_b[0], o_c_im_b[0])
    h = _odd_layer(h, mem2, bsz, seq, o_w_in[0], s5_fwd, s5_bwd, o_d[0], o_w_glu[0], o_mem_kv[0],
                   o_w_out[0], o_ln_g[0], o_ln_b[0])
    return h.reshape(bsz, seq, d)
```

```python
import functools
import math

import jax
import jax.numpy as jnp
from jax import lax
from jax.experimental import pallas as pl
from jax.experimental.pallas import tpu as pltpu

F32 = jnp.float32
BF16 = jnp.bfloat16

D_MODEL = 1024
CONV_DIM = 1024
CONV_WIDTH = 31
CONV_PAD = 15
MLA_HEADS = 8
QK_NOPE = 128
QK_ROPE = 64
V_DIM = 128
Q_LORA = 768
KV_LORA = 256
ROPE_THETA = 10000.0
MEM_HEADS = 4
MEM_HEAD_DIM = 128
MEM_DIM = 512
S5_DIM = 1024
S5_GROUP = 16
S5_GROUPS = 64
S5_STATE = 64
LN_EPS = 1e-5
RMS_EPS = 1e-6
DEPTH = 2
ALPHA = (2 * DEPTH) ** 0.25

LANE = 128
SUBLANE = 8
VMEM_LIMIT = 56 * 1024 * 1024

E_CONV_IN = 0
E_CONV_GATE = 2048
E_CQ = 3072
E_CKV = 3840
E_MLA_GATE = 4096
E_MEM_Q = 5120
E_MEM_GATE = 5632
E_KROPE = 6144
E_NPAD = 6656
O_SPLIT = 1024
O_GATE = 0
O_MEM_Q = 1024
O_MEM_GATE = 1536

S5_L = 8


def _cparams(sem):
    return pltpu.CompilerParams(dimension_semantics=sem, vmem_limit_bytes=VMEM_LIMIT)


def _silu(x):
    return x * jax.nn.sigmoid(x)


def _proj_body(a_ref, w_ref, *o_refs, bounds):
    y = jnp.dot(a_ref[...].astype(BF16), w_ref[...], preferred_element_type=F32)
    for o_ref, (lo, hi) in zip(o_refs, bounds):
        o_ref[...] = y[:, lo:hi].astype(o_ref.dtype)


def _proj(a, w, outs, tm, name):
    m, k = a.shape
    n = w.shape[1]
    tm = min(tm, m)
    assert m % tm == 0 and sum(width for width, _ in outs) == n
    bounds, lo = [], 0
    for width, _ in outs:
        bounds.append((lo, lo + width))
        lo += width
    res = pl.pallas_call(
        functools.partial(_proj_body, bounds=tuple(bounds)),
        grid=(m // tm,),
        in_specs=[pl.BlockSpec((tm, k), lambda i: (i, 0)),
                  pl.BlockSpec((k, n), lambda i: (0, 0))],
        out_specs=[pl.BlockSpec((tm, width), lambda i: (i, 0)) for width, _ in outs],
        out_shape=[jax.ShapeDtypeStruct((m, width), dt) for width, dt in outs],
        compiler_params=_cparams(("parallel",)),
        name=name,
    )(a, w)
    return res


CONV_HALO = 16
CONV_RC = 32
CONV_COPY_ROWS = 40


def _conv_body(m1, m2, p1, p2, n1, n2, gate, cw, cb, lg, lb, o_ref, slab, *, ts, tiles_per_seq):
    i = pl.program_id(0)
    first = (i % tiles_per_seq) == 0
    last = (i % tiles_per_seq) == tiles_per_seq - 1

    def glu(a, b):
        return a[...].astype(F32) * jax.nn.sigmoid(b[...].astype(F32))

    slab[0, pl.ds(CONV_HALO, ts), :] = glu(m1, m2)
    slab[0, pl.ds(0, CONV_HALO), :] = jnp.where(first, 0.0, glu(p1, p2))
    slab[0, pl.ds(CONV_HALO + ts, CONV_HALO), :] = jnp.where(last, 0.0, glu(n1, n2))

    span = ts + 2 * CONV_HALO - SUBLANE
    for r in range(1, SUBLANE):
        for c0 in range(0, span, CONV_COPY_ROWS):
            n = min(CONV_COPY_ROWS, span - c0)
            slab[r, pl.ds(c0, n), :] = slab[0, pl.ds(c0 + r, n), :]

    off = CONV_HALO - CONV_PAD
    for rc in range(ts // CONV_RC):
        base = rc * CONV_RC
        acc = jnp.zeros((CONV_RC, CONV_DIM), F32)
        for k in range(CONV_WIDTH):
            r = (off + k) % SUBLANE
            acc = acc + slab[r, pl.ds(base + off + k - r, CONV_RC), :] * cw[pl.ds(k, 1), :]
        dw = acc + cb[...]
        mu = jnp.mean(dw, axis=-1, keepdims=True)
        xc = dw - mu
        var = jnp.mean(xc * xc, axis=-1, keepdims=True)
        y = xc * lax.rsqrt(var + LN_EPS) * lg[...] + lb[...]
        res = _silu(y) * _silu(gate[pl.ds(base, CONV_RC), :].astype(F32))
        o_ref[pl.ds(base, CONV_RC), :] = res.astype(o_ref.dtype)


def _conv_branch(h, conv_w, conv_b, ln_g, ln_b, seq, ts=256):
    t = h.shape[0]
    nb = ts // CONV_HALO
    last_blk = t // CONV_HALO - 1
    c1, c2, cg = E_CONV_IN // CONV_DIM, E_CONV_IN // CONV_DIM + 1, E_CONV_GATE // CONV_DIM
    body = functools.partial(_conv_body, ts=ts, tiles_per_seq=seq // ts)
    vec = pl.BlockSpec((1, CONV_DIM), lambda i: (0, 0))
    return pl.pallas_call(
        body,
        grid=(t // ts,),
        in_specs=[
            pl.BlockSpec((ts, CONV_DIM), lambda i: (i, c1)),
            pl.BlockSpec((ts, CONV_DIM), lambda i: (i, c2)),
            pl.BlockSpec((CONV_HALO, CONV_DIM), lambda i: (jnp.maximum(i * nb - 1, 0), c1)),
            pl.BlockSpec((CONV_HALO, CONV_DIM), lambda i: (jnp.maximum(i * nb - 1, 0), c2)),
            pl.BlockSpec((CONV_HALO, CONV_DIM), lambda i: (jnp.minimum((i + 1) * nb, last_blk), c1)),
            pl.BlockSpec((CONV_HALO, CONV_DIM), lambda i: (jnp.minimum((i + 1) * nb, last_blk), c2)),
            pl.BlockSpec((ts, CONV_DIM), lambda i: (i, cg)),
            pl.BlockSpec((CONV_WIDTH, CONV_DIM), lambda i: (0, 0)),
            vec, vec, vec,
        ],
        out_specs=pl.BlockSpec((ts, CONV_DIM), lambda i: (i, 0)),
        out_shape=jax.ShapeDtypeStruct((t, CONV_DIM), BF16),
        scratch_shapes=[pltpu.VMEM((SUBLANE, ts + 2 * CONV_HALO, CONV_DIM), F32)],
        compiler_params=_cparams(("parallel",)),
        name="conv_branch",
    )(h, h, h, h, h, h, h, conv_w, conv_b, ln_g, ln_b)


def _rms(x, g):
    ms = jnp.mean(x * x, axis=-1, keepdims=True)
    return x * lax.rsqrt(ms + RMS_EPS) * g


def _mla_proj_body(cq, ckv, kr, cos, sin, qn_g, kvn_g, wq, wkv, q_ref, k_ref, v_ref, *, scale):
    c = cos[0]
    s = sin[0]
    nq = _rms(cq[...].astype(F32), qn_g[...]).astype(BF16)
    qf = jnp.dot(nq, wq[...], preferred_element_type=F32)
    nh = MLA_HEADS * QK_NOPE
    nr = MLA_HEADS * QK_ROPE
    c4 = jnp.concatenate([c] * (nr // LANE), axis=-1)
    s4 = jnp.concatenate([s] * (nr // LANE), axis=-1)
    qr = qf[:, nh:nh + nr] * c4 + qf[:, nh + nr:nh + 2 * nr] * s4
    nkv = _rms(ckv[...].astype(F32), kvn_g[...]).astype(BF16)
    kvf = jnp.dot(nkv, wkv[...], preferred_element_type=F32)
    krf = kr[...].astype(F32)
    kr_even = krf[:, 0:LANE] * c + krf[:, 2 * LANE:3 * LANE] * s
    kr_odd = krf[:, LANE:2 * LANE] * c + krf[:, 3 * LANE:4 * LANE] * s
    for h in range(MLA_HEADS):
        q_ref[0, h, :, 0:QK_NOPE] = (qf[:, h * QK_NOPE:(h + 1) * QK_NOPE] * scale).astype(BF16)
        p = h // 2
        q_ref[0, h, :, QK_NOPE:2 * QK_NOPE] = (qr[:, p * LANE:(p + 1) * LANE] * scale).astype(BF16)
        k_ref[0, h, :, 0:QK_NOPE] = kvf[:, h * QK_NOPE:(h + 1) * QK_NOPE].astype(BF16)
        k_ref[0, h, :, QK_NOPE:2 * QK_NOPE] = (kr_even if h % 2 == 0 else kr_odd).astype(BF16)
        v_ref[0, h, :, :] = kvf[:, nh + h * V_DIM:nh + (h + 1) * V_DIM].astype(BF16)


def _mla_proj(h, cos128, sin128, q_norm, kv_norm, wq, wkv, bsz, seq, tm=1024):
    nt = seq // tm
    scale = (QK_NOPE + QK_ROPE) ** -0.5 * math.log2(math.e)
    body = functools.partial(_mla_proj_body, scale=scale)
    hd = 2 * QK_NOPE
    return pl.pallas_call(
        body,
        grid=(bsz, nt),
        in_specs=[
            pl.BlockSpec((tm, Q_LORA), lambda b, i: (b * nt + i, E_CQ // Q_LORA)),
            pl.BlockSpec((tm, KV_LORA), lambda b, i: (b * nt + i, E_CKV // KV_LORA)),
            pl.BlockSpec((tm, 4 * LANE), lambda b, i: (b * nt + i, E_KROPE // (4 * LANE))),
            pl.BlockSpec((1, tm, LANE), lambda b, i: (b, i, 0)),
            pl.BlockSpec((1, tm, LANE), lambda b, i: (b, i, 0)),
            pl.BlockSpec((1, Q_LORA), lambda b, i: (0, 0)),
            pl.BlockSpec((1, KV_LORA), lambda b, i: (0, 0)),
            pl.BlockSpec(wq.shape, lambda b, i: (0, 0)),
            pl.BlockSpec(wkv.shape, lambda b, i: (0, 0)),
        ],
        out_specs=[
            pl.BlockSpec((1, MLA_HEADS, tm, hd), lambda b, i: (b, 0, i, 0)),
            pl.BlockSpec((1, MLA_HEADS, tm, hd), lambda b, i: (b, 0, i, 0)),
            pl.BlockSpec((1, MLA_HEADS, tm, V_DIM), lambda b, i: (b, 0, i, 0)),
        ],
        out_shape=[
            jax.ShapeDtypeStruct((bsz, MLA_HEADS, seq, hd), BF16),
            jax.ShapeDtypeStruct((bsz, MLA_HEADS, seq, hd), BF16),
            jax.ShapeDtypeStruct((bsz, MLA_HEADS, seq, V_DIM), BF16),
        ],
        compiler_params=_cparams(("parallel", "parallel")),
        name="mla_proj",
    )(h, h, h, cos128, sin128, q_norm, kv_norm, wq, wkv)


def _flash_body(q_ref, k_ref, v_ref, g_ref, o_ref, *, tk, nk, unroll):
    q = q_ref[0, 0]
    tq = q.shape[0]

    def step(c, carry):
        m, l, acc = carry
        start = pl.multiple_of(c * tk, tk)
        ks = k_ref[0, 0, pl.ds(start, tk), :]
        vs = v_ref[0, 0, pl.ds(start, tk), :]
        s = lax.dot_general(q, ks, (((1,), (1,)), ((), ())), preferred_element_type=F32)
        m_new = jnp.maximum(m, jnp.max(s, axis=-1, keepdims=True))
        alpha = jnp.exp2(m - m_new)
        p = jnp.exp2(s - m_new)
        l = alpha * l + jnp.sum(p, axis=-1, keepdims=True)
        acc = alpha * acc + jnp.dot(p.astype(BF16), vs, preferred_element_type=F32)
        return m_new, l, acc

    m0 = jnp.full((tq, 1), -jnp.inf, F32)
    l0 = jnp.zeros((tq, 1), F32)
    a0 = jnp.zeros((tq, V_DIM), F32)
    m, l, acc = lax.fori_loop(0, nk, step, (m0, l0, a0), unroll=unroll)
    o_ref[...] = (acc / l * _silu(g_ref[...].astype(F32))).astype(o_ref.dtype)


def _mla_attention(q, k, v, h, tq=1024, tk=1024, unroll=True):
    bsz, heads, seq, hd = q.shape
    nq = seq // tq
    body = functools.partial(_flash_body, tk=tk, nk=seq // tk, unroll=unroll)
    gcol = E_MLA_GATE // V_DIM
    return pl.pallas_call(
        body,
        grid=(bsz, heads, nq),
        in_specs=[
            pl.BlockSpec((1, 1, tq, hd), lambda b, hh, i: (b, hh, i, 0)),
            pl.BlockSpec((1, 1, seq, hd), lambda b, hh, i: (b, hh, 0, 0)),
            pl.BlockSpec((1, 1, seq, V_DIM), lambda b, hh, i: (b, hh, 0, 0)),
            pl.BlockSpec((tq, V_DIM), lambda b, hh, i: (b * nq + i, gcol + hh)),
        ],
        out_specs=pl.BlockSpec((tq, V_DIM), lambda b, hh, i: (b * nq + i, hh)),
        out_shape=jax.ShapeDtypeStruct((bsz * seq, heads * V_DIM), BF16),
        compiler_params=_cparams(("parallel", "parallel", "arbitrary")),
        name="mla_attention",
    )(q, k, v, h)


def _mem_attn_body(q_ref, g_ref, kv_ref, o_ref, *, scale):
    for hh in range(MEM_HEADS):
        lo, hi = hh * MEM_HEAD_DIM, (hh + 1) * MEM_HEAD_DIM
        kk = kv_ref[0, :, lo:hi]
        vv = kv_ref[0, :, MEM_DIM + lo:MEM_DIM + hi]
        s = lax.dot_general(q_ref[:, lo:hi], kk, (((1,), (1,)), ((), ())),
                            preferred_element_type=F32) * scale
        m = jnp.max(s, axis=-1, keepdims=True)
        p = jnp.exp(s - m)
        l = jnp.sum(p, axis=-1, keepdims=True)
        o = jnp.dot(p.astype(BF16), vv, preferred_element_type=F32) / l
        o_ref[:, lo:hi] = (o * _silu(g_ref[:, lo:hi].astype(F32))).astype(o_ref.dtype)


def _mem_attention(h, memkv, q_off, g_off, seq, tm=1024):
    t = h.shape[0]
    nt = seq // tm
    mlen = memkv.shape[1]
    body = functools.partial(_mem_attn_body, scale=MEM_HEAD_DIM ** -0.5)
    return pl.pallas_call(
        body,
        grid=(t // tm,),
        in_specs=[
            pl.BlockSpec((tm, MEM_DIM), lambda i: (i, q_off // MEM_DIM)),
            pl.BlockSpec((tm, MEM_DIM), lambda i: (i, g_off // MEM_DIM)),
            pl.BlockSpec((1, mlen, 2 * MEM_DIM), lambda i: (i // nt, 0, 0)),
        ],
        out_specs=pl.BlockSpec((tm, MEM_DIM), lambda i: (i, 0)),
        out_shape=jax.ShapeDtypeStruct((t, MEM_DIM), BF16),
        compiler_params=_cparams(("parallel",)),
        name="mem_attention",
    )(h, h, memkv)


def _out_ln_body(*refs, nparts):
    parts = refs[:nparts]
    ws = refs[nparts:2 * nparts]
    x_ref, g_ref, b_ref, o_ref = refs[2 * nparts:]
    y = jnp.dot(parts[0][...], ws[0][...], preferred_element_type=F32)
    for p, w in zip(parts[1:], ws[1:]):
        y = y + jnp.dot(p[...], w[...], preferred_element_type=F32)
    z = ALPHA * x_ref[...] + y
    mu = jnp.mean(z, axis=-1, keepdims=True)
    zc = z - mu
    var = jnp.mean(zc * zc, axis=-1, keepdims=True)
    o_ref[...] = zc * lax.rsqrt(var + LN_EPS) * g_ref[...] + b_ref[...]


def _out_ln(parts, w, x, g, b, tm=1024, name="out_ln"):
    t = x.shape[0]
    n = len(parts)
    body = functools.partial(_out_ln_body, nparts=n)
    in_specs = [pl.BlockSpec((tm, p.shape[1]), lambda i: (i, 0)) for p in parts]
    row = 0
    for p in parts:
        width = p.shape[1]
        assert row % width == 0
        in_specs.append(pl.BlockSpec((width, D_MODEL), lambda i, blk=row // width: (blk, 0)))
        row += width
    in_specs += [pl.BlockSpec((tm, D_MODEL), lambda i: (i, 0)),
                 pl.BlockSpec((1, D_MODEL), lambda i: (0, 0)),
                 pl.BlockSpec((1, D_MODEL), lambda i: (0, 0))]
    return pl.pallas_call(
        body,
        grid=(t // tm,),
        in_specs=in_specs,
        out_specs=pl.BlockSpec((tm, D_MODEL), lambda i: (i, 0)),
        out_shape=jax.ShapeDtypeStruct((t, D_MODEL), F32),
        compiler_params=_cparams(("parallel",)),
        name=name,
    )(*parts, *([w] * n), x, g, b)


S5_C2 = 8
PW_A8, PW_A64, PW_A512, PW_A1024, PW_A2048 = 0, 1, 2, 3, 4
PW_A64K = 5
PW_A8K = 13
PW_ROWS = 24
S5_PAD = 8


def _cmul(pr, pi, xr, xi):
    return pr * xr - pi * xi, pr * xi + pi * xr


def _s5_scan(s_ref, x_ref, pw_ref, rev):
    half = 512
    n0 = 8 * S5_C2
    n1 = S5_C2

    def pw(row):
        v = pw_ref[pl.ds(row, 1), :]
        return v[:, :half], v[:, half:]

    def blk0(k):
        return (7 - k if rev else k) * n0

    def blk1(k):
        return (7 - k if rev else k) * n1

    def ld(ref, start, n):
        v = ref[pl.ds(start, n), :]
        return v[:, :half], v[:, half:]

    def st(ref, start, n, re, im):
        ref[pl.ds(start, n), 0:half] = re
        ref[pl.ds(start, n), half:2 * half] = im

    a8r, a8i = pw(PW_A8)
    for k in range(1, 8):
        pr_, pi_ = ld(s_ref, blk0(k - 1), n0)
        cr, ci = ld(s_ref, blk0(k), n0)
        mr, mi = _cmul(a8r, a8i, pr_, pi_)
        st(s_ref, blk0(k), n0, cr + mr, ci + mi)
    g0 = blk0(7)
    a64r, a64i = pw(PW_A64)
    for k in range(1, 8):
        pr_, pi_ = ld(s_ref, g0 + blk1(k - 1), n1)
        cr, ci = ld(s_ref, g0 + blk1(k), n1)
        mr, mi = _cmul(a64r, a64i, pr_, pi_)
        st(s_ref, g0 + blk1(k), n1, cr + mr, ci + mi)
    hr, hi = ld(s_ref, g0 + blk1(7), n1)
    c2 = lax.broadcasted_iota(jnp.int32, (n1, half), 0)
    if rev:
        c2 = (S5_C2 - 1) - c2
    for d, row in ((1, PW_A512), (2, PW_A1024), (4, PW_A2048)):
        ar, ai = pw(row)
        sh = (n1 - d) if rev else d
        sr = pltpu.roll(hr, sh, 0)
        si = pltpu.roll(hi, sh, 0)
        mr, mi = _cmul(ar, ai, sr, si)
        keep = c2 >= d
        hr = hr + jnp.where(keep, mr, 0.0)
        hi = hi + jnp.where(keep, mi, 0.0)
    sh1 = (n1 - 1) if rev else 1
    p3r = jnp.where(c2 >= 1, pltpu.roll(hr, sh1, 0), 0.0)
    p3i = jnp.where(c2 >= 1, pltpu.roll(hi, sh1, 0), 0.0)
    for k1 in range(8):
        if k1 == 0:
            er, ei = p3r, p3i
        else:
            ar, ai = pw(PW_A64K + k1 - 1)
            mr, mi = _cmul(ar, ai, p3r, p3i)
            qr, qi = ld(s_ref, g0 + blk1(k1 - 1), n1)
            er, ei = qr + mr, qi + mi
        for k0 in range(8):
            dst = blk0(k0) + blk1(k1)
            if k0 == 0:
                st(x_ref, dst, n1, er, ei)
            else:
                ar, ai = pw(PW_A8K + k0)
                mr, mi = _cmul(ar, ai, er, ei)
                qr, qi = ld(s_ref, blk0(k0 - 1) + blk1(k1), n1)
                st(x_ref, dst, n1, qr + mr, qi + mi)


def _lane_select(d_log2, tt):
    kk = lax.broadcasted_iota(jnp.int32, (LANE, LANE), 0)
    col = tt * LANE + lax.broadcasted_iota(jnp.int32, (LANE, LANE), 1)
    src = lax.shift_left(lax.shift_right_logical(col, d_log2 + 3), d_log2) + (col & ((1 << d_log2) - 1))
    return jnp.where(kk == src, 1.0, 0.0).astype(BF16)


def _expand_blockdiag(comp, m_ref, row0, sels, *, row_b_log2, d_log2):
    rows = comp.shape[0]
    rowg = lax.shift_right_logical(lax.broadcasted_iota(jnp.int32, (rows, LANE), 0), row_b_log2) & 7
    coln = lax.broadcasted_iota(jnp.int32, (rows, LANE), 1)
    for tt in range(8):
        piece = jnp.dot(comp, sels[tt], preferred_element_type=F32)
        colg = lax.shift_right_logical(tt * LANE + coln, d_log2) & 7
        m_ref[pl.ds(row0, rows), tt * LANE:(tt + 1) * LANE] = jnp.where(
            rowg == colg, piece, 0.0).astype(BF16)


def _expand_toeplitz(ct_ref, tw):
    gc = S5_GROUP
    k16 = lax.broadcasted_iota(jnp.int32, (gc, LANE), 0)
    n16 = lax.broadcasted_iota(jnp.int32, (gc, LANE), 1)
    sel = jnp.where(k16 == (n16 & (gc - 1)), 1.0, 0.0).astype(BF16)
    rowg = lax.shift_right_logical(lax.broadcasted_iota(jnp.int32, (LANE, LANE), 0), 4)
    colg = lax.shift_right_logical(lax.broadcasted_iota(jnp.int32, (LANE, LANE), 1), 4)
    keep = rowg == colg
    for idx in range(2 * S5_L - 1):
        comp = ct_ref[idx, 0].astype(BF16)
        blk = jnp.where(keep, jnp.dot(comp, sel, preferred_element_type=F32), 0.0).astype(BF16)
        lag = idx if idx < S5_L else (S5_L - 1) - idx
        for l in range(S5_L):
            lp = l + lag
            if 0 <= lp < S5_L:
                tw[l * LANE:(l + 1) * LANE, lp * LANE:(lp + 1) * LANE] = blk


def _s5_body(u_ref, ct_ref, cwf_ref, cwb_ref, cvf_ref, cvb_ref, pwf_ref, pwb_ref, d_ref, o_ref,
             tw, wf, wb, vf, vb, stage, xcat, sbuf, xf, xb, *, seq):
    n1 = S5_C2
    blk_rows = seq // S5_C2
    pitch = blk_rows + S5_PAD

    @pl.when(pl.program_id(1) == 0)
    def _():
        _expand_toeplitz(ct_ref, tw)
        sel_w = [_lane_select(6, tt) for tt in range(8)]
        for l in range(S5_L):
            _expand_blockdiag(cwf_ref[l, 0, 0], wf, l * LANE, sel_w, row_b_log2=4, d_log2=6)
            _expand_blockdiag(cwb_ref[l, 0, 0], wb, l * LANE, sel_w, row_b_log2=4, d_log2=6)
        sel_v = [_lane_select(4, tt) for tt in range(8)]
        for r in range(2):
            _expand_blockdiag(cvf_ref[0, r, 0], vf, r * 512, sel_v, row_b_log2=6, d_log2=4)
            _expand_blockdiag(cvb_ref[0, r, 0], vb, r * 512, sel_v, row_b_log2=6, d_log2=4)

    def pieces():
        for c0 in range(8):
            for c1 in range(8):
                for l in range(S5_L):
                    yield (c0 * 8 * n1 + c1 * n1, l * LANE,
                           pl.ds(c1 * 64 + c0 * 8 + l, n1, stride=pitch))

    for c2 in range(S5_C2):
        stage[pl.ds(c2 * pitch, blk_rows), :] = u_ref[pl.ds(c2 * blk_rows, blk_rows), :]
    for r0, l0, tok in pieces():
        xcat[pl.ds(r0, n1), l0:l0 + LANE] = stage[tok, :]
    xb16 = xcat[...].astype(BF16)
    sbuf[...] = jnp.dot(xb16, wf[...], preferred_element_type=F32)
    _s5_scan(sbuf, xf, pwf_ref, False)
    sbuf[...] = jnp.dot(xb16, wb[...], preferred_element_type=F32)
    _s5_scan(sbuf, xb, pwb_ref, True)
    y = jnp.dot(xb16, tw[...], preferred_element_type=F32)
    y = y + jnp.dot(xf[...].astype(BF16), vf[...], preferred_element_type=F32)
    y = y + jnp.dot(xb[...].astype(BF16), vb[...], preferred_element_type=F32)
    dd = jnp.concatenate([d_ref[...]] * S5_L, axis=-1)
    sbuf[...] = jax.nn.gelu(y + dd * xcat[...])
    for r0, l0, tok in pieces():
        stage[tok, :] = sbuf[pl.ds(r0, n1), l0:l0 + LANE]
    for c2 in range(S5_C2):
        o_ref[pl.ds(c2 * blk_rows, blk_rows), :] = stage[pl.ds(c2 * pitch, blk_rows), :].astype(o_ref.dtype)


def _s5_branch(u, ops, d, bsz, seq):
    t = u.shape[0]
    ct, cw, cv, pw = ops
    ntile = S5_DIM // LANE
    rows = seq // S5_L
    wide = S5_L * LANE
    body = functools.partial(_s5_body, seq=seq)

    def cw_spec(di):
        return pl.BlockSpec((S5_L, 1, 1, LANE, LANE), lambda j, b: (0, di, j, 0, 0))

    def cv_spec(di):
        return pl.BlockSpec((1, 2, 1, 512, LANE), lambda j, b: (di, 0, j, 0, 0))

    def pw_spec(di):
        return pl.BlockSpec((PW_ROWS, wide), lambda j, b: (0, di * ntile + j))

    return pl.pallas_call(
        body,
        grid=(ntile, bsz),
        in_specs=[pl.BlockSpec((seq, LANE), lambda j, b: (b, j)),
                  pl.BlockSpec((2 * S5_L - 1, 1, LANE, S5_GROUP), lambda j, b: (0, j, 0, 0)),
                  cw_spec(0), cw_spec(1), cv_spec(0), cv_spec(1), pw_spec(0), pw_spec(1),
                  pl.BlockSpec((1, LANE), lambda j, b: (0, j))],
        out_specs=pl.BlockSpec((seq, LANE), lambda j, b: (b, j)),
        out_shape=jax.ShapeDtypeStruct((t, S5_DIM), BF16),
        scratch_shapes=([pltpu.VMEM((wide, wide), BF16)] * 5
                        + [pltpu.VMEM((seq + S5_C2 * S5_PAD, LANE), F32)]
                        + [pltpu.VMEM((rows, wide), F32)] * 4),
        compiler_params=_cparams(("parallel", "arbitrary")),
        name="s5_branch",
    )(u, ct, cw, cw, cv, cv, pw, pw, d)


def _s5_prep(fwd, bwd):
    G, P, C, L = S5_GROUPS, S5_STATE, S5_GROUP, S5_L
    nt = G // 8
    a_re, a_im, log_dt, b_re, b_im, c_re, c_im = (jnp.stack([f, b]) for f, b in zip(fwd, bwd))
    lam_re = jnp.minimum(a_re, -1e-4)
    lam_im = a_im
    dt = jnp.exp(log_dt)[..., None]
    mag = jnp.exp(lam_re * dt)
    lb_re = mag * jnp.cos(lam_im * dt)
    lb_im = mag * jnp.sin(lam_im * dt)
    den = lam_re * lam_re + lam_im * lam_im
    nr = lb_re - 1.0
    f_re = (nr * lam_re + lb_im * lam_im) / den
    f_im = (lb_im * lam_re - nr * lam_im) / den
    bt_re = jnp.swapaxes(b_re, -1, -2)
    bt_im = jnp.swapaxes(b_im, -1, -2)
    bbt_re = f_re[:, :, None, :] * bt_re - f_im[:, :, None, :] * bt_im
    bbt_im = f_re[:, :, None, :] * bt_im + f_im[:, :, None, :] * bt_re

    def power(n):
        n = jnp.asarray(n, F32).reshape(-1, 1, 1, 1)
        m = jnp.exp(n * (lam_re * dt))
        ang = n * (lam_im * dt)
        return m * jnp.cos(ang), m * jnp.sin(ang)

    pr, pi = power(jnp.arange(L + 1))

    ca_re = c_re[None] * pr[:, :, :, None, :] - c_im[None] * pi[:, :, :, None, :]
    ca_im = c_re[None] * pi[:, :, :, None, :] + c_im[None] * pr[:, :, :, None, :]
    kern = jnp.sum(bbt_re[None, :, :, :, None, :] * ca_re[:L, :, :, None, :, :]
                   - bbt_im[None, :, :, :, None, :] * ca_im[:L, :, :, None, :, :], axis=-1)
    k_all = jnp.concatenate([(kern[0, 0] + kern[0, 1])[None], kern[1:, 0], kern[1:, 1]], axis=0)
    ct = k_all.reshape(2 * L - 1, nt, 8 * C, C)

    prw = jnp.stack([pr[L - 1::-1, 0], pr[:L, 1]], axis=1)
    piw = jnp.stack([pi[L - 1::-1, 0], pi[:L, 1]], axis=1)
    aa = jnp.concatenate([prw, prw], axis=-1)[:, :, :, None, :]
    ab = jnp.concatenate([-piw, piw], axis=-1)[:, :, :, None, :]
    x1 = jnp.concatenate([bbt_re, bbt_im], axis=-1)[None]
    x2 = jnp.concatenate([bbt_im, bbt_re], axis=-1)[None]
    cw = (aa * x1 + ab * x2).astype(BF16).reshape(L, 2, nt, 8 * C, 2 * P)

    ctr_re = jnp.swapaxes(c_re, -1, -2)
    ctr_im = jnp.swapaxes(c_im, -1, -2)
    cat_re = ctr_re[None] * pr[..., None] - ctr_im[None] * pi[..., None]
    cat_im = ctr_re[None] * pi[..., None] + ctr_im[None] * pr[..., None]

    def lanes_lc(x, di, order):
        return jnp.concatenate([x[n, di] for n in order], axis=-1)

    fo, bo = range(1, L + 1), range(L, 0, -1)
    cv = jnp.stack([jnp.stack([lanes_lc(cat_re, 0, fo), -lanes_lc(cat_im, 0, fo)]),
                    jnp.stack([lanes_lc(cat_re, 1, bo), -lanes_lc(cat_im, 1, bo)])])
    cv = cv.astype(BF16).reshape(2, 2, nt, 8 * P, L * C)

    ns = ([8.0, 64.0, 512.0, 1024.0, 2048.0] + [64.0 * (k + 1) for k in range(8)]
          + [8.0 * k for k in range(8)] + [0.0] * (PW_ROWS - 21))
    qr, qi = power(ns)
    pw = jnp.concatenate([qr.reshape(PW_ROWS, 2, nt, 8 * P), qi.reshape(PW_ROWS, 2, nt, 8 * P)],
                         axis=-1).reshape(PW_ROWS, 2 * nt * 2 * 8 * P)
    return ct, cw, cv, pw


def _glu_mm_body(a_ref, w_ref, g_ref, o_ref):
    z = jnp.dot(a_ref[...], w_ref[...], preferred_element_type=F32)
    n = z.shape[1] // 2
    o_ref[...] = (z[:, :n] * jax.nn.sigmoid(z[:, n:]) * _silu(g_ref[...].astype(F32))).astype(o_ref.dtype)


def _glu_matmul(a, w, h2, tm=1024):
    t, k = a.shape
    n = w.shape[1] // 2
    return pl.pallas_call(
        _glu_mm_body,
        grid=(t // tm,),
        in_specs=[pl.BlockSpec((tm, k), lambda i: (i, 0)),
                  pl.BlockSpec((k, 2 * n), lambda i: (0, 0)),
                  pl.BlockSpec((tm, n), lambda i: (i, O_GATE // n))],
        out_specs=pl.BlockSpec((tm, n), lambda i: (i, 0)),
        out_shape=jax.ShapeDtypeStruct((t, n), BF16),
        compiler_params=_cparams(("parallel",)),
        name="s5_glu",
    )(a, w, h2)


def _swap_halves(w, width):
    k, n = w.shape
    w = w.reshape(k, n // width, 2, width // 2)
    return w[:, :, ::-1, :].reshape(k, n)


def _even_in_weight(w_in):
    k = w_in.shape[0]
    conv_in, conv_gate, c_q, c_kv, k_rope, mla_gate, mem_q, mem_gate = jnp.split(
        w_in, [2048, 3072, 3840, 4096, 4160, 5184, 5696], axis=1)
    z = jnp.zeros((k, QK_ROPE), w_in.dtype)
    krs = _swap_halves(k_rope, QK_ROPE)
    slots = [k_rope, z, z, k_rope, krs, z, z, krs]
    return jnp.concatenate([conv_in, conv_gate, c_q, c_kv, mla_gate, mem_q, mem_gate] + slots,
                           axis=1).astype(BF16)


def _uq_weight(w_uq):
    k = w_uq.shape[0]
    w = w_uq.reshape(k, MLA_HEADS, QK_NOPE + QK_ROPE)
    nope = w[:, :, :QK_NOPE].reshape(k, MLA_HEADS * QK_NOPE)
    rope = w[:, :, QK_NOPE:].reshape(k, MLA_HEADS * QK_ROPE)
    return jnp.concatenate([nope, rope, _swap_halves(rope, QK_ROPE)], axis=1).astype(BF16)


def _ukv_weight(w_ukv):
    k = w_ukv.shape[0]
    w = w_ukv.reshape(k, MLA_HEADS, QK_NOPE + V_DIM)
    kn = w[:, :, :QK_NOPE].reshape(k, MLA_HEADS * QK_NOPE)
    vv = w[:, :, QK_NOPE:].reshape(k, MLA_HEADS * V_DIM)
    return jnp.concatenate([kn, vv], axis=1).astype(BF16)


def _rope_tables(positions):
    inv_freq = ROPE_THETA ** (-jnp.arange(0, QK_ROPE, 2, dtype=F32) / QK_ROPE)
    half = QK_ROPE // 2
    inv4 = jnp.tile(inv_freq, LANE // half)
    sign = jnp.tile(jnp.concatenate([-jnp.ones((half,), F32), jnp.ones((half,), F32)]),
                    LANE // QK_ROPE)
    ang = positions.astype(F32)[..., None] * inv4
    return jnp.cos(ang), jnp.sin(ang) * sign


def _row(v):
    return v.reshape(1, -1)


def _even_layer(x2, mem2, cos128, sin128, bsz, seq, w_in, conv_w, conv_b, conv_ln_g, conv_ln_b,
                q_norm, w_uq, kv_norm, w_ukv, w_mem_kv, w_out, ln_g, ln_b):
    (h,) = _proj(x2, _even_in_weight(w_in), [(E_NPAD, BF16)], 512, "even_in_proj")
    a_out = _conv_branch(h, conv_w.reshape(CONV_WIDTH, CONV_DIM), _row(conv_b), _row(conv_ln_g),
                         _row(conv_ln_b), seq)
    q, k, v = _mla_proj(h, cos128, sin128, _row(q_norm), _row(kv_norm), _uq_weight(w_uq),
                        _ukv_weight(w_ukv), bsz, seq)
    b_out = _mla_attention(q, k, v, h)
    (memkv,) = _proj(mem2, w_mem_kv.astype(BF16), [(2 * MEM_DIM, BF16)], 512, "even_mem_kv")
    memkv = memkv.reshape(bsz, -1, 2 * MEM_DIM)
    m_out = _mem_attention(h, memkv, E_MEM_Q, E_MEM_GATE, seq)
    return _out_ln([a_out, b_out, m_out], w_out.astype(BF16), x2, _row(ln_g), _row(ln_b),
                   name="even_out_ln")


def _odd_layer(x2, mem2, bsz, seq, w_in, s5_fwd, s5_bwd, s5_d, w_glu, w_mem_kv, w_out, ln_g, ln_b):
    n_in = w_in.shape[1]
    u, h2 = _proj(x2, w_in.astype(BF16), [(O_SPLIT, F32), (n_in - O_SPLIT, BF16)], 1024,
                  "odd_in_proj")
    g = _s5_branch(u, _s5_prep(s5_fwd, s5_bwd), _row(s5_d), bsz, seq)
    c_out = _glu_matmul(g, w_glu.astype(BF16), h2)
    (memkv,) = _proj(mem2, w_mem_kv.astype(BF16), [(2 * MEM_DIM, BF16)], 512, "odd_mem_kv")
    memkv = memkv.reshape(bsz, -1, 2 * MEM_DIM)
    m_out = _mem_attention(h2, memkv, O_MEM_Q, O_MEM_GATE, seq)
    return _out_ln([c_out, m_out], w_out.astype(BF16), x2, _row(ln_g), _row(ln_b),
                   name="odd_out_ln")


def kernel(x, mem, positions, e_w_in, e_conv_w, e_conv_b, e_conv_ln_g, e_conv_ln_b, e_q_norm, e_w_uq, e_kv_norm, e_w_ukv, e_mem_kv, e_w_out, e_ln_g, e_ln_b, o_w_in, o_a_re_f, o_a_im_f, o_log_dt_f, o_b_re_f, o_b_im_f, o_c_re_f, o_c_im_f, o_a_re_b, o_a_im_b, o_log_dt_b, o_b_re_b, o_b_im_b, o_c_re_b, o_c_im_b, o_d, o_w_glu, o_mem_kv, o_w_out, o_ln_g, o_ln_b):
    bsz, seq, d = x.shape
    cos128, sin128 = _rope_tables(positions)
    x2 = x.reshape(bsz * seq, d)
    mem2 = mem.reshape(-1, d)
    h = _even_layer(x2, mem2, cos128, sin128, bsz, seq, e_w_in[0], e_conv_w[0], e_conv_b[0],
                    e_conv_ln_g[0], e_conv_ln_b[0], e_q_norm[0], e_w_uq[0], e_kv_norm[0],
                    e_w_ukv[0], e_mem_kv[0], e_w_out[0], e_ln_g[0], e_ln_b[0])
    s5_fwd = (o_a_re_f[0], o_a_im_f[0], o_log_dt_f[0], o_b_re_f[0], o_b_im_f[0], o_c_re_f[0], o_c_im_f[0])
    s5_bwd = (o_a_re_b[0], o_a_im_b[0], o_log_dt_b[0], o_b_re_b[0], o_b_im_b[0], o_c_re_b[0], o_c_im_b[0])
    h = _odd_layer(h, mem2, bsz, seq, o_w_in[0], s5_fwd, s5_bwd, o_d[0], o_w_glu[0], o_mem_kv[0],
                   o_w_out[0], o_ln_g[0], o_ln_b[0])
    return h.reshape(bsz, seq, d)
```

```python
import functools
import math

import jax
import jax.numpy as jnp
from jax import lax
from jax.experimental import pallas as pl
from jax.experimental.pallas import tpu as pltpu

F32 = jnp.float32
BF16 = jnp.bfloat16

D_MODEL = 1024
CONV_DIM = 1024
CONV_WIDTH = 31
CONV_PAD = 15
MLA_HEADS = 8
QK_NOPE = 128
QK_ROPE = 64
V_DIM = 128
Q_LORA = 768
KV_LORA = 256
ROPE_THETA = 10000.0
MEM_HEADS = 4
MEM_HEAD_DIM = 128
MEM_DIM = 512
S5_DIM = 1024
S5_GROUP = 16
S5_GROUPS = 64
S5_STATE = 64
LN_EPS = 1e-5
RMS_EPS = 1e-6
DEPTH = 2
ALPHA = (2 * DEPTH) ** 0.25

LANE = 128
SUBLANE = 8
VMEM_LIMIT = 56 * 1024 * 1024

E_CONV_IN = 0
E_CONV_GATE = 2048
E_CQ = 3072
E_CKV = 3840
E_MLA_GATE = 4096
E_MEM_Q = 5120
E_MEM_GATE = 5632
E_KROPE = 6144
E_NPAD = 6656
O_SPLIT = 1024
O_GATE = 0
O_MEM_Q = 1024
O_MEM_GATE = 1536

S5_L = 8


def _cparams(sem):
    return pltpu.CompilerParams(dimension_semantics=sem, vmem_limit_bytes=VMEM_LIMIT)


def _silu(x):
    return x * jax.nn.sigmoid(x)


def _proj_body(a_ref, w_ref, *o_refs, bounds):
    y = jnp.dot(a_ref[...].astype(BF16), w_ref[...], preferred_element_type=F32)
    for o_ref, (lo, hi) in zip(o_refs, bounds):
        o_ref[...] = y[:, lo:hi].astype(o_ref.dtype)


def _proj(a, w, outs, tm, name):
    m, k = a.shape
    n = w.shape[1]
    tm = min(tm, m)
    assert m % tm == 0 and sum(width for width, _ in outs) == n
    bounds, lo = [], 0
    for width, _ in outs:
        bounds.append((lo, lo + width))
        lo += width
    res = pl.pallas_call(
        functools.partial(_proj_body, bounds=tuple(bounds)),
        grid=(m // tm,),
        in_specs=[pl.BlockSpec((tm, k), lambda i: (i, 0)),
                  pl.BlockSpec((k, n), lambda i: (0, 0))],
        out_specs=[pl.BlockSpec((tm, width), lambda i: (i, 0)) for width, _ in outs],
        out_shape=[jax.ShapeDtypeStruct((m, width), dt) for width, dt in outs],
        compiler_params=_cparams(("parallel",)),
        name=name,
    )(a, w)
    return res


CONV_HALO = 16
CONV_RC = 32


def _conv_body(m1, m2, p1, p2, n1, n2, gate, cw, cb, lg, lb, o_ref, slab, *, ts, tiles_per_seq):
    i = pl.program_id(0)
    first = (i % tiles_per_seq) == 0
    last = (i % tiles_per_seq) == tiles_per_seq - 1

    def glu(a, b):
        return a[...].astype(F32) * jax.nn.sigmoid(b[...].astype(F32))

    slab[0, pl.ds(CONV_HALO, ts), :] = glu(m1, m2)
    slab[0, pl.ds(0, CONV_HALO), :] = jnp.where(first, 0.0, glu(p1, p2))
    slab[0, pl.ds(CONV_HALO + ts, CONV_HALO), :] = jnp.where(last, 0.0, glu(n1, n2))

    rows = ts + 2 * CONV_HALO
    for lt in range(CONV_DIM // LANE):
        cols = slice(lt * LANE, (lt + 1) * LANE)
        base = slab[0, :, cols]
        for r in range(1, SUBLANE):
            slab[r, :, cols] = pltpu.roll(base, rows - r, 0)

    off = CONV_HALO - CONV_PAD
    for rc in range(ts // CONV_RC):
        base = rc * CONV_RC
        acc = jnp.zeros((CONV_RC, CONV_DIM), F32)
        for k in range(CONV_WIDTH):
            r = (off + k) % SUBLANE
            wk = jnp.concatenate([cw[k]] * (CONV_RC // SUBLANE), axis=0)
            acc = acc + slab[r, pl.ds(base + off + k - r, CONV_RC), :] * wk
        dw = acc + cb[...]
        mu = jnp.mean(dw, axis=-1, keepdims=True)
        xc = dw - mu
        var = jnp.mean(xc * xc, axis=-1, keepdims=True)
        y = xc * lax.rsqrt(var + LN_EPS) * lg[...] + lb[...]
        res = _silu(y) * _silu(gate[pl.ds(base, CONV_RC), :].astype(F32))
        o_ref[pl.ds(base, CONV_RC), :] = res.astype(o_ref.dtype)


def _conv_branch(h, conv_w, conv_b, ln_g, ln_b, seq, ts=256):
    t = h.shape[0]
    nb = ts // CONV_HALO
    last_blk = t // CONV_HALO - 1
    c1, c2, cg = E_CONV_IN // CONV_DIM, E_CONV_IN // CONV_DIM + 1, E_CONV_GATE // CONV_DIM
    body = functools.partial(_conv_body, ts=ts, tiles_per_seq=seq // ts)
    vec = pl.BlockSpec((1, CONV_DIM), lambda i: (0, 0))
    return pl.pallas_call(
        body,
        grid=(t // ts,),
        in_specs=[
            pl.BlockSpec((ts, CONV_DIM), lambda i: (i, c1)),
            pl.BlockSpec((ts, CONV_DIM), lambda i: (i, c2)),
            pl.BlockSpec((CONV_HALO, CONV_DIM), lambda i: (jnp.maximum(i * nb - 1, 0), c1)),
            pl.BlockSpec((CONV_HALO, CONV_DIM), lambda i: (jnp.maximum(i * nb - 1, 0), c2)),
            pl.BlockSpec((CONV_HALO, CONV_DIM), lambda i: (jnp.minimum((i + 1) * nb, last_blk), c1)),
            pl.BlockSpec((CONV_HALO, CONV_DIM), lambda i: (jnp.minimum((i + 1) * nb, last_blk), c2)),
            pl.BlockSpec((ts, CONV_DIM), lambda i: (i, cg)),
            pl.BlockSpec((CONV_WIDTH, SUBLANE, CONV_DIM), lambda i: (0, 0, 0)),
            vec, vec, vec,
        ],
        out_specs=pl.BlockSpec((ts, CONV_DIM), lambda i: (i, 0)),
        out_shape=jax.ShapeDtypeStruct((t, CONV_DIM), BF16),
        scratch_shapes=[pltpu.VMEM((SUBLANE, ts + 2 * CONV_HALO, CONV_DIM), F32)],
        compiler_params=_cparams(("parallel",)),
        name="conv_branch",
    )(h, h, h, h, h, h, h, conv_w, conv_b, ln_g, ln_b)


def _rms(x, g):
    ms = jnp.mean(x * x, axis=-1, keepdims=True)
    return x * lax.rsqrt(ms + RMS_EPS) * g


def _mla_proj_body(cq, ckv, kr, cos, sin, qn_g, kvn_g, wq, wkv, q_ref, k_ref, v_ref, *, scale):
    c = cos[0]
    s = sin[0]
    nq = _rms(cq[...].astype(F32), qn_g[...]).astype(BF16)
    qf = jnp.dot(nq, wq[...], preferred_element_type=F32)
    nh = MLA_HEADS * QK_NOPE
    nr = MLA_HEADS * QK_ROPE
    c4 = jnp.concatenate([c] * (nr // LANE), axis=-1)
    s4 = jnp.concatenate([s] * (nr // LANE), axis=-1)
    qr = qf[:, nh:nh + nr] * c4 + qf[:, nh + nr:nh + 2 * nr] * s4
    nkv = _rms(ckv[...].astype(F32), kvn_g[...]).astype(BF16)
    kvf = jnp.dot(nkv, wkv[...], preferred_element_type=F32)
    krf = kr[...].astype(F32)
    kr_even = krf[:, 0:LANE] * c + krf[:, 2 * LANE:3 * LANE] * s
    kr_odd = krf[:, LANE:2 * LANE] * c + krf[:, 3 * LANE:4 * LANE] * s
    for h in range(MLA_HEADS):
        q_ref[0, h, :, 0:QK_NOPE] = (qf[:, h * QK_NOPE:(h + 1) * QK_NOPE] * scale).astype(BF16)
        p = h // 2
        q_ref[0, h, :, QK_NOPE:2 * QK_NOPE] = (qr[:, p * LANE:(p + 1) * LANE] * scale).astype(BF16)
        k_ref[0, h, :, 0:QK_NOPE] = kvf[:, h * QK_NOPE:(h + 1) * QK_NOPE].astype(BF16)
        k_ref[0, h, :, QK_NOPE:2 * QK_NOPE] = (kr_even if h % 2 == 0 else kr_odd).astype(BF16)
        v_ref[0, h, :, :] = kvf[:, nh + h * V_DIM:nh + (h + 1) * V_DIM].astype(BF16)


def _mla_proj(h, cos128, sin128, q_norm, kv_norm, wq, wkv, bsz, seq, tm=1024):
    nt = seq // tm
    scale = (QK_NOPE + QK_ROPE) ** -0.5 * math.log2(math.e)
    body = functools.partial(_mla_proj_body, scale=scale)
    hd = 2 * QK_NOPE
    return pl.pallas_call(
        body,
        grid=(bsz, nt),
        in_specs=[
            pl.BlockSpec((tm, Q_LORA), lambda b, i: (b * nt + i, E_CQ // Q_LORA)),
            pl.BlockSpec((tm, KV_LORA), lambda b, i: (b * nt + i, E_CKV // KV_LORA)),
            pl.BlockSpec((tm, 4 * LANE), lambda b, i: (b * nt + i, E_KROPE // (4 * LANE))),
            pl.BlockSpec((1, tm, LANE), lambda b, i: (b, i, 0)),
            pl.BlockSpec((1, tm, LANE), lambda b, i: (b, i, 0)),
            pl.BlockSpec((1, Q_LORA), lambda b, i: (0, 0)),
            pl.BlockSpec((1, KV_LORA), lambda b, i: (0, 0)),
            pl.BlockSpec(wq.shape, lambda b, i: (0, 0)),
            pl.BlockSpec(wkv.shape, lambda b, i: (0, 0)),
        ],
        out_specs=[
            pl.BlockSpec((1, MLA_HEADS, tm, hd), lambda b, i: (b, 0, i, 0)),
            pl.BlockSpec((1, MLA_HEADS, tm, hd), lambda b, i: (b, 0, i, 0)),
            pl.BlockSpec((1, MLA_HEADS, tm, V_DIM), lambda b, i: (b, 0, i, 0)),
        ],
        out_shape=[
            jax.ShapeDtypeStruct((bsz, MLA_HEADS, seq, hd), BF16),
            jax.ShapeDtypeStruct((bsz, MLA_HEADS, seq, hd), BF16),
            jax.ShapeDtypeStruct((bsz, MLA_HEADS, seq, V_DIM), BF16),
        ],
        compiler_params=_cparams(("parallel", "parallel")),
        name="mla_proj",
    )(h, h, h, cos128, sin128, q_norm, kv_norm, wq, wkv)


def _flash_body(q_ref, k_ref, v_ref, g_ref, o_ref, *, tk, nk, unroll):
    q = q_ref[0, 0]
    tq = q.shape[0]

    def step(c, carry):
        m, l, acc = carry
        start = pl.multiple_of(c * tk, tk)
        ks = k_ref[0, 0, pl.ds(start, tk), :]
        vs = v_ref[0, 0, pl.ds(start, tk), :]
        s = lax.dot_general(q, ks, (((1,), (1,)), ((), ())), preferred_element_type=F32)
        m_new = jnp.maximum(m, jnp.max(s, axis=-1, keepdims=True))
        alpha = jnp.exp2(m - m_new)
        p = jnp.exp2(s - m_new)
        l = alpha * l + jnp.sum(p, axis=-1, keepdims=True)
        acc = alpha * acc + jnp.dot(p.astype(BF16), vs, preferred_element_type=F32)
        return m_new, l, acc

    m0 = jnp.full((tq, 1), -jnp.inf, F32)
    l0 = jnp.zeros((tq, 1), F32)
    a0 = jnp.zeros((tq, V_DIM), F32)
    m, l, acc = lax.fori_loop(0, nk, step, (m0, l0, a0), unroll=unroll)
    o_ref[...] = (acc / l * _silu(g_ref[...].astype(F32))).astype(o_ref.dtype)


def _mla_attention(q, k, v, h, tq=1024, tk=1024, unroll=True):
    bsz, heads, seq, hd = q.shape
    nq = seq // tq
    body = functools.partial(_flash_body, tk=tk, nk=seq // tk, unroll=unroll)
    gcol = E_MLA_GATE // V_DIM
    return pl.pallas_call(
        body,
        grid=(bsz, heads, nq),
        in_specs=[
            pl.BlockSpec((1, 1, tq, hd), lambda b, hh, i: (b, hh, i, 0)),
            pl.BlockSpec((1, 1, seq, hd), lambda b, hh, i: (b, hh, 0, 0)),
            pl.BlockSpec((1, 1, seq, V_DIM), lambda b, hh, i: (b, hh, 0, 0)),
            pl.BlockSpec((tq, V_DIM), lambda b, hh, i: (b * nq + i, gcol + hh)),
        ],
        out_specs=pl.BlockSpec((tq, V_DIM), lambda b, hh, i: (b * nq + i, hh)),
        out_shape=jax.ShapeDtypeStruct((bsz * seq, heads * V_DIM), BF16),
        compiler_params=_cparams(("parallel", "parallel", "arbitrary")),
        name="mla_attention",
    )(q, k, v, h)


def _mem_attn_body(q_ref, g_ref, kv_ref, o_ref, *, scale):
    for hh in range(MEM_HEADS):
        lo, hi = hh * MEM_HEAD_DIM, (hh + 1) * MEM_HEAD_DIM
        kk = kv_ref[0, :, lo:hi]
        vv = kv_ref[0, :, MEM_DIM + lo:MEM_DIM + hi]
        s = lax.dot_general(q_ref[:, lo:hi], kk, (((1,), (1,)), ((), ())),
                            preferred_element_type=F32) * scale
        m = jnp.max(s, axis=-1, keepdims=True)
        p = jnp.exp(s - m)
        l = jnp.sum(p, axis=-1, keepdims=True)
        o = jnp.dot(p.astype(BF16), vv, preferred_element_type=F32) / l
        o_ref[:, lo:hi] = (o * _silu(g_ref[:, lo:hi].astype(F32))).astype(o_ref.dtype)


def _mem_attention(h, memkv, q_off, g_off, seq, tm=1024):
    t = h.shape[0]
    nt = seq // tm
    mlen = memkv.shape[1]
    body = functools.partial(_mem_attn_body, scale=MEM_HEAD_DIM ** -0.5)
    return pl.pallas_call(
        body,
        grid=(t // tm,),
        in_specs=[
            pl.BlockSpec((tm, MEM_DIM), lambda i: (i, q_off // MEM_DIM)),
            pl.BlockSpec((tm, MEM_DIM), lambda i: (i, g_off // MEM_DIM)),
            pl.BlockSpec((1, mlen, 2 * MEM_DIM), lambda i: (i // nt, 0, 0)),
        ],
        out_specs=pl.BlockSpec((tm, MEM_DIM), lambda i: (i, 0)),
        out_shape=jax.ShapeDtypeStruct((t, MEM_DIM), BF16),
        compiler_params=_cparams(("parallel",)),
        name="mem_attention",
    )(h, h, memkv)


def _out_ln_body(*refs, nparts):
    parts = refs[:nparts]
    w_ref, x_ref, g_ref, b_ref, o_ref = refs[nparts:]
    cat = jnp.concatenate([p[...] for p in parts], axis=1)
    y = jnp.dot(cat, w_ref[...], preferred_element_type=F32)
    z = ALPHA * x_ref[...] + y
    mu = jnp.mean(z, axis=-1, keepdims=True)
    zc = z - mu
    var = jnp.mean(zc * zc, axis=-1, keepdims=True)
    o_ref[...] = zc * lax.rsqrt(var + LN_EPS) * g_ref[...] + b_ref[...]


def _out_ln(parts, w, x, g, b, tm=1024, name="out_ln"):
    t = x.shape[0]
    n = len(parts)
    body = functools.partial(_out_ln_body, nparts=n)
    assert sum(p.shape[1] for p in parts) == w.shape[0]
    in_specs = [pl.BlockSpec((tm, p.shape[1]), lambda i: (i, 0)) for p in parts]
    in_specs += [pl.BlockSpec(w.shape, lambda i: (0, 0)),
                 pl.BlockSpec((tm, D_MODEL), lambda i: (i, 0)),
                 pl.BlockSpec((1, D_MODEL), lambda i: (0, 0)),
                 pl.BlockSpec((1, D_MODEL), lambda i: (0, 0))]
    return pl.pallas_call(
        body,
        grid=(t // tm,),
        in_specs=in_specs,
        out_specs=pl.BlockSpec((tm, D_MODEL), lambda i: (i, 0)),
        out_shape=jax.ShapeDtypeStruct((t, D_MODEL), F32),
        compiler_params=_cparams(("parallel",)),
        name=name,
    )(*parts, w, x, g, b)


S5_C2 = 8
PW_A8, PW_A64, PW_A512, PW_A1024, PW_A2048 = 0, 1, 2, 3, 4
PW_A64K = 5
PW_A8K = 13
PW_ROWS = 24
S5_PAD = 8


def _cmul(pr, pi, xr, xi):
    return pr * xr - pi * xi, pr * xi + pi * xr


def _s5_scan(s_ref, x_ref, pw_ref, rev):
    half = 512
    n0 = 8 * S5_C2
    n1 = S5_C2

    def pw(row):
        v = pw_ref[pl.ds(row, 1), :]
        return v[:, :half], v[:, half:]

    def blk0(k):
        return (7 - k if rev else k) * n0

    def blk1(k):
        return (7 - k if rev else k) * n1

    def ld(ref, start, n):
        v = ref[pl.ds(start, n), :]
        return v[:, :half], v[:, half:]

    def st(ref, start, n, re, im):
        ref[pl.ds(start, n), 0:half] = re
        ref[pl.ds(start, n), half:2 * half] = im

    a8r, a8i = pw(PW_A8)
    for k in range(1, 8):
        pr_, pi_ = ld(s_ref, blk0(k - 1), n0)
        cr, ci = ld(s_ref, blk0(k), n0)
        mr, mi = _cmul(a8r, a8i, pr_, pi_)
        st(s_ref, blk0(k), n0, cr + mr, ci + mi)
    g0 = blk0(7)
    a64r, a64i = pw(PW_A64)
    for k in range(1, 8):
        pr_, pi_ = ld(s_ref, g0 + blk1(k - 1), n1)
        cr, ci = ld(s_ref, g0 + blk1(k), n1)
        mr, mi = _cmul(a64r, a64i, pr_, pi_)
        st(s_ref, g0 + blk1(k), n1, cr + mr, ci + mi)
    hr, hi = ld(s_ref, g0 + blk1(7), n1)
    c2 = lax.broadcasted_iota(jnp.int32, (n1, half), 0)
    if rev:
        c2 = (S5_C2 - 1) - c2
    for d, row in ((1, PW_A512), (2, PW_A1024), (4, PW_A2048)):
        ar, ai = pw(row)
        sh = (n1 - d) if rev else d
        sr = pltpu.roll(hr, sh, 0)
        si = pltpu.roll(hi, sh, 0)
        mr, mi = _cmul(ar, ai, sr, si)
        keep = c2 >= d
        hr = hr + jnp.where(keep, mr, 0.0)
        hi = hi + jnp.where(keep, mi, 0.0)
    sh1 = (n1 - 1) if rev else 1
    p3r = jnp.where(c2 >= 1, pltpu.roll(hr, sh1, 0), 0.0)
    p3i = jnp.where(c2 >= 1, pltpu.roll(hi, sh1, 0), 0.0)
    for k1 in range(8):
        if k1 == 0:
            er, ei = p3r, p3i
        else:
            ar, ai = pw(PW_A64K + k1 - 1)
            mr, mi = _cmul(ar, ai, p3r, p3i)
            qr, qi = ld(s_ref, g0 + blk1(k1 - 1), n1)
            er, ei = qr + mr, qi + mi
        for k0 in range(8):
            dst = blk0(k0) + blk1(k1)
            if k0 == 0:
                st(x_ref, dst, n1, er, ei)
            else:
                ar, ai = pw(PW_A8K + k0)
                mr, mi = _cmul(ar, ai, er, ei)
                qr, qi = ld(s_ref, blk0(k0 - 1) + blk1(k1), n1)
                st(x_ref, dst, n1, qr + mr, qi + mi)


def _lane_select(d_log2, tt):
    kk = lax.broadcasted_iota(jnp.int32, (LANE, LANE), 0)
    col = tt * LANE + lax.broadcasted_iota(jnp.int32, (LANE, LANE), 1)
    src = lax.shift_left(lax.shift_right_logical(col, d_log2 + 3), d_log2) + (col & ((1 << d_log2) - 1))
    return jnp.where(kk == src, 1.0, 0.0).astype(BF16)


def _expand_blockdiag(comp, m_ref, row0, sels, *, row_b_log2, d_log2):
    rows = comp.shape[0]
    rowg = lax.shift_right_logical(lax.broadcasted_iota(jnp.int32, (rows, LANE), 0), row_b_log2) & 7
    coln = lax.broadcasted_iota(jnp.int32, (rows, LANE), 1)
    for tt in range(8):
        piece = jnp.dot(comp, sels[tt], preferred_element_type=F32)
        colg = lax.shift_right_logical(tt * LANE + coln, d_log2) & 7
        m_ref[pl.ds(row0, rows), tt * LANE:(tt + 1) * LANE] = jnp.where(
            rowg == colg, piece, 0.0).astype(BF16)


_NT = (((1,), (1,)), ((), ()))


def _expand_toeplitz(bx_ref, cax_ref, tw):
    rowg = lax.shift_right_logical(lax.broadcasted_iota(jnp.int32, (LANE, LANE), 0), 4)
    colg = lax.shift_right_logical(lax.broadcasted_iota(jnp.int32, (LANE, LANE), 1), 4)
    keep = rowg == colg

    def kern(n, d):
        return lax.dot_general(bx_ref[d, 0], cax_ref[n, d, 0], _NT,
                               precision=lax.Precision.HIGHEST, preferred_element_type=F32)

    for lag in range(-(S5_L - 1), S5_L):
        if lag == 0:
            k = kern(0, 0) + kern(0, 1)
        else:
            k = kern(lag, 0) if lag > 0 else kern(-lag, 1)
        blk = jnp.where(keep, k, 0.0).astype(BF16)
        for l in range(S5_L):
            lp = l + lag
            if 0 <= lp < S5_L:
                tw[l * LANE:(l + 1) * LANE, lp * LANE:(lp + 1) * LANE] = blk


def _expand_readout(cax_ref, v_ref, d, lags):
    half = 8 * S5_STATE
    row = lax.broadcasted_iota(jnp.int32, (half, LANE), 0)
    lane = lax.broadcasted_iota(jnp.int32, (half, LANE), 1)
    rowg = lax.shift_right_logical(row, 6)
    colg = lax.shift_right_logical(lane, 4)
    for r in range(2):
        pick = jnp.where(((row & (S5_STATE - 1)) + r * S5_STATE) == lane, 1.0, 0.0).astype(BF16)
        for lp, n in enumerate(lags):
            piece = lax.dot_general(pick, cax_ref[n, d, 0].astype(BF16), _NT, preferred_element_type=F32)
            v_ref[r * half:(r + 1) * half, lp * LANE:(lp + 1) * LANE] = jnp.where(
                rowg == colg, piece, 0.0).astype(BF16)


def _s5_body(u_ref, bx_ref, cax_ref, cw_ref, pwf_ref, pwb_ref, d_ref, o_ref,
             tw, wf, wb, vf, vb, stage, xcat, sbuf, xf, xb, *, seq):
    n1 = S5_C2
    blk_rows = seq // S5_C2
    pitch = blk_rows + S5_PAD

    @pl.when(pl.program_id(1) == 0)
    def _():
        _expand_toeplitz(bx_ref, cax_ref, tw)
        sel_w = [_lane_select(6, tt) for tt in range(8)]
        for l in range(S5_L):
            _expand_blockdiag(cw_ref[l, 0, 0], wf, l * LANE, sel_w, row_b_log2=4, d_log2=6)
            _expand_blockdiag(cw_ref[l, 1, 0], wb, l * LANE, sel_w, row_b_log2=4, d_log2=6)
        _expand_readout(cax_ref, vf, 0, [lp + 1 for lp in range(S5_L)])
        _expand_readout(cax_ref, vb, 1, [S5_L - lp for lp in range(S5_L)])

    def pieces():
        for c0 in range(8):
            for c1 in range(8):
                for l in range(S5_L):
                    yield (c0 * 8 * n1 + c1 * n1, l * LANE,
                           pl.ds(c1 * 64 + c0 * 8 + l, n1, stride=pitch))

    for c2 in range(S5_C2):
        stage[pl.ds(c2 * pitch, blk_rows), :] = u_ref[pl.ds(c2 * blk_rows, blk_rows), :]
    for r0, l0, tok in pieces():
        xcat[pl.ds(r0, n1), l0:l0 + LANE] = stage[tok, :]
    xb16 = xcat[...].astype(BF16)
    sbuf[...] = jnp.dot(xb16, wf[...], preferred_element_type=F32)
    _s5_scan(sbuf, xf, pwf_ref, False)
    sbuf[...] = jnp.dot(xb16, wb[...], preferred_element_type=F32)
    _s5_scan(sbuf, xb, pwb_ref, True)
    y = jnp.dot(xb16, tw[...], preferred_element_type=F32)
    y = y + jnp.dot(xf[...].astype(BF16), vf[...], preferred_element_type=F32)
    y = y + jnp.dot(xb[...].astype(BF16), vb[...], preferred_element_type=F32)
    dd = jnp.concatenate([d_ref[...]] * S5_L, axis=-1)
    sbuf[...] = jax.nn.gelu(y + dd * xcat[...])
    for r0, l0, tok in pieces():
        stage[tok, :] = sbuf[pl.ds(r0, n1), l0:l0 + LANE]
    for c2 in range(S5_C2):
        o_ref[pl.ds(c2 * blk_rows, blk_rows), :] = stage[pl.ds(c2 * pitch, blk_rows), :].astype(o_ref.dtype)


def _s5_branch(u, ops, d, bsz, seq):
    t = u.shape[0]
    bx, cax, cw, pw = ops
    ntile = S5_DIM // LANE
    rows = seq // S5_L
    wide = S5_L * LANE
    body = functools.partial(_s5_body, seq=seq)

    def pw_spec(di):
        return pl.BlockSpec((PW_ROWS, wide), lambda j, b: (0, di * ntile + j))

    return pl.pallas_call(
        body,
        grid=(ntile, bsz),
        in_specs=[pl.BlockSpec((seq, LANE), lambda j, b: (b, j)),
                  pl.BlockSpec((2, 1, LANE, LANE), lambda j, b: (0, j, 0, 0)),
                  pl.BlockSpec((S5_L + 1, 2, 1, LANE, LANE), lambda j, b: (0, 0, j, 0, 0)),
                  pl.BlockSpec((S5_L, 2, 1, LANE, LANE), lambda j, b: (0, 0, j, 0, 0)),
                  pw_spec(0), pw_spec(1),
                  pl.BlockSpec((1, LANE), lambda j, b: (0, j))],
        out_specs=pl.BlockSpec((seq, LANE), lambda j, b: (b, j)),
        out_shape=jax.ShapeDtypeStruct((t, S5_DIM), BF16),
        scratch_shapes=([pltpu.VMEM((wide, wide), BF16)] * 5
                        + [pltpu.VMEM((seq + S5_C2 * S5_PAD, LANE), F32)]
                        + [pltpu.VMEM((rows, wide), F32)] * 4),
        compiler_params=_cparams(("parallel", "arbitrary")),
        name="s5_branch",
    )(u, bx, cax, cw, pw, pw, d)


def _s5_prep(fwd, bwd):
    G, P, C, L = S5_GROUPS, S5_STATE, S5_GROUP, S5_L
    nt = G // 8
    a_re, a_im, log_dt, b_re, b_im, c_re, c_im = (jnp.stack([f, b]) for f, b in zip(fwd, bwd))
    lam_re = jnp.minimum(a_re, -1e-4)
    lam_im = a_im
    dt = jnp.exp(log_dt)[..., None]
    mag = jnp.exp(lam_re * dt)
    lb_re = mag * jnp.cos(lam_im * dt)
    lb_im = mag * jnp.sin(lam_im * dt)
    den = lam_re * lam_re + lam_im * lam_im
    nr = lb_re - 1.0
    f_re = (nr * lam_re + lb_im * lam_im) / den
    f_im = (lb_im * lam_re - nr * lam_im) / den
    bt_re = jnp.swapaxes(b_re, -1, -2)
    bt_im = jnp.swapaxes(b_im, -1, -2)
    bbt_re = f_re[:, :, None, :] * bt_re - f_im[:, :, None, :] * bt_im
    bbt_im = f_re[:, :, None, :] * bt_im + f_im[:, :, None, :] * bt_re

    def power(n):
        n = jnp.asarray(n, F32).reshape(-1, 1, 1, 1)
        m = jnp.exp(n * (lam_re * dt))
        ang = n * (lam_im * dt)
        return m * jnp.cos(ang), m * jnp.sin(ang)

    pr, pi = power(jnp.arange(L + 1))

    pr2 = jnp.concatenate([pr, -pi], axis=-1)[:, :, :, None, :]
    pi2 = jnp.concatenate([-pi, -pr], axis=-1)[:, :, :, None, :]
    c_ri = jnp.concatenate([c_re, c_re], axis=-1)[None]
    c_ii = jnp.concatenate([c_im, c_im], axis=-1)[None]
    cax = (c_ri * pr2 + c_ii * pi2).reshape(L + 1, 2, nt, 8 * C, 2 * P)
    bx = jnp.concatenate([bbt_re, bbt_im], axis=-1)

    prw = jnp.stack([pr[L - 1::-1, 0], pr[:L, 1]], axis=1)
    piw = jnp.stack([pi[L - 1::-1, 0], pi[:L, 1]], axis=1)
    aa = jnp.concatenate([prw, prw], axis=-1)[:, :, :, None, :]
    ab = jnp.concatenate([-piw, piw], axis=-1)[:, :, :, None, :]
    x2 = jnp.concatenate([bbt_im, bbt_re], axis=-1)[None]
    cw = (aa * bx[None] + ab * x2).astype(BF16).reshape(L, 2, nt, 8 * C, 2 * P)
    bx = bx.reshape(2, nt, 8 * C, 2 * P)

    ns = ([8.0, 64.0, 512.0, 1024.0, 2048.0] + [64.0 * (k + 1) for k in range(8)]
          + [8.0 * k for k in range(8)] + [0.0] * (PW_ROWS - 21))
    qr, qi = power(ns)
    pw = jnp.concatenate([qr.reshape(PW_ROWS, 2, nt, 8 * P), qi.reshape(PW_ROWS, 2, nt, 8 * P)],
                         axis=-1).reshape(PW_ROWS, 2 * nt * 2 * 8 * P)
    return bx, cax, cw, pw


GLU_COLS = 256


def _glu_mm_body(a_ref, w_ref, g_ref, o_ref):
    a = a_ref[...]
    n = o_ref.shape[1]
    for c0 in range(0, n, GLU_COLS):
        z1 = jnp.dot(a, w_ref[:, c0:c0 + GLU_COLS], preferred_element_type=F32)
        z2 = jnp.dot(a, w_ref[:, n + c0:n + c0 + GLU_COLS], preferred_element_type=F32)
        gate = _silu(g_ref[:, c0:c0 + GLU_COLS].astype(F32))
        o_ref[:, c0:c0 + GLU_COLS] = (z1 * jax.nn.sigmoid(z2) * gate).astype(o_ref.dtype)


def _glu_matmul(a, w, h2, tm=1024):
    t, k = a.shape
    n = w.shape[1] // 2
    return pl.pallas_call(
        _glu_mm_body,
        grid=(t // tm,),
        in_specs=[pl.BlockSpec((tm, k), lambda i: (i, 0)),
                  pl.BlockSpec((k, 2 * n), lambda i: (0, 0)),
                  pl.BlockSpec((tm, n), lambda i: (i, O_GATE // n))],
        out_specs=pl.BlockSpec((tm, n), lambda i: (i, 0)),
        out_shape=jax.ShapeDtypeStruct((t, n), BF16),
        compiler_params=_cparams(("parallel",)),
        name="s5_glu",
    )(a, w, h2)


def _swap_halves(w, width):
    k, n = w.shape
    w = w.reshape(k, n // width, 2, width // 2)
    return w[:, :, ::-1, :].reshape(k, n)


def _even_in_weight(w_in):
    k = w_in.shape[0]
    conv_in, conv_gate, c_q, c_kv, k_rope, mla_gate, mem_q, mem_gate = jnp.split(
        w_in, [2048, 3072, 3840, 4096, 4160, 5184, 5696], axis=1)
    z = jnp.zeros((k, QK_ROPE), w_in.dtype)
    krs = _swap_halves(k_rope, QK_ROPE)
    slots = [k_rope, z, z, k_rope, krs, z, z, krs]
    return jnp.concatenate([conv_in, conv_gate, c_q, c_kv, mla_gate, mem_q, mem_gate] + slots,
                           axis=1).astype(BF16)


def _uq_weight(w_uq):
    k = w_uq.shape[0]
    w = w_uq.reshape(k, MLA_HEADS, QK_NOPE + QK_ROPE)
    nope = w[:, :, :QK_NOPE].reshape(k, MLA_HEADS * QK_NOPE)
    rope = w[:, :, QK_NOPE:].reshape(k, MLA_HEADS * QK_ROPE)
    return jnp.concatenate([nope, rope, _swap_halves(rope, QK_ROPE)], axis=1).astype(BF16)


def _ukv_weight(w_ukv):
    k = w_ukv.shape[0]
    w = w_ukv.reshape(k, MLA_HEADS, QK_NOPE + V_DIM)
    kn = w[:, :, :QK_NOPE].reshape(k, MLA_HEADS * QK_NOPE)
    vv = w[:, :, QK_NOPE:].reshape(k, MLA_HEADS * V_DIM)
    return jnp.concatenate([kn, vv], axis=1).astype(BF16)


def _rope_tables(positions):
    inv_freq = ROPE_THETA ** (-jnp.arange(0, QK_ROPE, 2, dtype=F32) / QK_ROPE)
    half = QK_ROPE // 2
    inv4 = jnp.tile(inv_freq, LANE // half)
    sign = jnp.tile(jnp.concatenate([-jnp.ones((half,), F32), jnp.ones((half,), F32)]),
                    LANE // QK_ROPE)
    ang = positions.astype(F32)[..., None] * inv4
    return jnp.cos(ang), jnp.sin(ang) * sign


def _row(v):
    return v.reshape(1, -1)


def _even_layer(x2, mem2, cos128, sin128, bsz, seq, w_in, conv_w, conv_b, conv_ln_g, conv_ln_b,
                q_norm, w_uq, kv_norm, w_ukv, w_mem_kv, w_out, ln_g, ln_b):
    (h,) = _proj(x2, _even_in_weight(w_in), [(E_NPAD, BF16)], 512, "even_in_proj")
    taps = jnp.broadcast_to(conv_w.reshape(CONV_WIDTH, 1, CONV_DIM), (CONV_WIDTH, SUBLANE, CONV_DIM))
    a_out = _conv_branch(h, taps, _row(conv_b), _row(conv_ln_g),
                         _row(conv_ln_b), seq)
    q, k, v = _mla_proj(h, cos128, sin128, _row(q_norm), _row(kv_norm), _uq_weight(w_uq),
                        _ukv_weight(w_ukv), bsz, seq)
    b_out = _mla_attention(q, k, v, h)
    (memkv,) = _proj(mem2, w_mem_kv.astype(BF16), [(2 * MEM_DIM, BF16)], 512, "even_mem_kv")
    memkv = memkv.reshape(bsz, -1, 2 * MEM_DIM)
    m_out = _mem_attention(h, memkv, E_MEM_Q, E_MEM_GATE, seq)
    return _out_ln([a_out, b_out, m_out], w_out.astype(BF16), x2, _row(ln_g), _row(ln_b),
                   name="even_out_ln")


def _odd_layer(x2, mem2, bsz, seq, w_in, s5_fwd, s5_bwd, s5_d, w_glu, w_mem_kv, w_out, ln_g, ln_b):
    n_in = w_in.shape[1]
    u, h2 = _proj(x2, w_in.astype(BF16), [(O_SPLIT, F32), (n_in - O_SPLIT, BF16)], 1024,
                  "odd_in_proj")
    g = _s5_branch(u, _s5_prep(s5_fwd, s5_bwd), _row(s5_d), bsz, seq)
    c_out = _glu_matmul(g, w_glu.astype(BF16), h2)
    (memkv,) = _proj(mem2, w_mem_kv.astype(BF16), [(2 * MEM_DIM, BF16)], 512, "odd_mem_kv")
    memkv = memkv.reshape(bsz, -1, 2 * MEM_DIM)
    m_out = _mem_attention(h2, memkv, O_MEM_Q, O_MEM_GATE, seq)
    return _out_ln([c_out, m_out], w_out.astype(BF16), x2, _row(ln_g), _row(ln_b),
                   name="odd_out_ln")


def kernel(x, mem, positions, e_w_in, e_conv_w, e_conv_b, e_conv_ln_g, e_conv_ln_b, e_q_norm, e_w_uq, e_kv_norm, e_w_ukv, e_mem_kv, e_w_out, e_ln_g, e_ln_b, o_w_in, o_a_re_f, o_a_im_f, o_log_dt_f, o_b_re_f, o_b_im_f, o_c_re_f, o_c_im_f, o_a_re_b, o_a_im_b, o_log_dt_b, o_b_re_b, o_b_im_b, o_c_re_b, o_c_im_b, o_d, o_w_glu, o_mem_kv, o_w_out, o_ln_g, o_ln_b):
    bsz, seq, d = x.shape
    cos128, sin128 = _rope_tables(positions)
    x2 = x.reshape(bsz * seq, d)
    mem2 = mem.reshape(-1, d)
    h = _even_layer(x2, mem2, cos128, sin128, bsz, seq, e_w_in[0], e_conv_w[0], e_conv_b[0],
                    e_conv_ln_g[0], e_conv_ln_b[0], e_q_norm[0], e_w_uq[0], e_kv_norm[0],
                    e_w_ukv[0], e_mem_kv[0], e_w_out[0], e_ln_g[0], e_ln_b[0])
    s5_fwd = (o_a_re_f[0], o_a_im_f[0], o_log_dt_f[0], o_b_re_f[0], o_b_im_f[0], o_c_re_f[0], o_c_im_f[0])
    s5_bwd = (o_a_re_b[0], o_a_im_b[0], o_log_dt_b[0], o_b_re_b[0], o_b_im_b[0], o_c_re_b[0], o_c_im_b[0])
    h = _odd_layer(h, mem2, bsz, seq, o_w_in[0], s5_fwd, s5_bwd, o_d[0], o_w_glu[0], o_mem_kv[0],
                   o_w_out[0], o_ln_g[0], o_ln_b[0])
    return h.reshape(bsz, seq, d)
```

```python
import functools
import math

import jax
import jax.numpy as jnp
from jax import lax
from jax.experimental import pallas as pl
from jax.experimental.pallas import tpu as pltpu

F32 = jnp.float32
BF16 = jnp.bfloat16

D_MODEL = 1024
CONV_DIM = 1024
CONV_WIDTH = 31
CONV_PAD = 15
MLA_HEADS = 8
QK_NOPE = 128
QK_ROPE = 64
V_DIM = 128
Q_LORA = 768
KV_LORA = 256
ROPE_THETA = 10000.0
MEM_HEADS = 4
MEM_HEAD_DIM = 128
MEM_DIM = 512
S5_DIM = 1024
S5_GROUP = 16
S5_GROUPS = 64
S5_STATE = 64
LN_EPS = 1e-5
RMS_EPS = 1e-6
DEPTH = 2
ALPHA = (2 * DEPTH) ** 0.25

LANE = 128
SUBLANE = 8
VMEM_LIMIT = 56 * 1024 * 1024

E_CONV_IN = 0
E_CONV_GATE = 2048
E_CQ = 3072
E_CKV = 3840
E_MLA_GATE = 4096
E_MEM_Q = 5120
E_MEM_GATE = 5632
E_KROPE = 6144
E_NPAD = 6272
O_SPLIT = 1024
O_GATE = 0
O_MEM_Q = 1024
O_MEM_GATE = 1536

S5_L = 8


def _cparams(sem):
    return pltpu.CompilerParams(dimension_semantics=sem, vmem_limit_bytes=VMEM_LIMIT)


def _silu(x):
    return x * jax.nn.sigmoid(x)


def _proj_body(a_ref, w_ref, *o_refs, bounds):
    y = jnp.dot(a_ref[...].astype(BF16), w_ref[...], preferred_element_type=F32)
    for o_ref, (lo, hi) in zip(o_refs, bounds):
        o_ref[...] = y[:, lo:hi].astype(o_ref.dtype)


def _proj(a, w, outs, tm, name):
    m, k = a.shape
    n = w.shape[1]
    tm = min(tm, m)
    assert m % tm == 0 and sum(width for width, _ in outs) == n
    bounds, lo = [], 0
    for width, _ in outs:
        bounds.append((lo, lo + width))
        lo += width
    res = pl.pallas_call(
        functools.partial(_proj_body, bounds=tuple(bounds)),
        grid=(m // tm,),
        in_specs=[pl.BlockSpec((tm, k), lambda i: (i, 0)),
                  pl.BlockSpec((k, n), lambda i: (0, 0))],
        out_specs=[pl.BlockSpec((tm, width), lambda i: (i, 0)) for width, _ in outs],
        out_shape=[jax.ShapeDtypeStruct((m, width), dt) for width, dt in outs],
        compiler_params=_cparams(("parallel",)),
        name=name,
    )(a, w)
    return res


CONV_HALO = 16
CONV_RC = 32


def _conv_body(m1, m2, p1, p2, n1, n2, gate, cw, cb, lg, lb, o_ref, slab, *, ts, tiles_per_seq):
    i = pl.program_id(0)
    first = (i % tiles_per_seq) == 0
    last = (i % tiles_per_seq) == tiles_per_seq - 1

    def glu(a, b):
        return a[...].astype(F32) * jax.nn.sigmoid(b[...].astype(F32))

    slab[0, pl.ds(CONV_HALO, ts), :] = glu(m1, m2)
    slab[0, pl.ds(0, CONV_HALO), :] = jnp.where(first, 0.0, glu(p1, p2))
    slab[0, pl.ds(CONV_HALO + ts, CONV_HALO), :] = jnp.where(last, 0.0, glu(n1, n2))

    rows = ts + 2 * CONV_HALO
    for lt in range(CONV_DIM // LANE):
        cols = slice(lt * LANE, (lt + 1) * LANE)
        base = slab[0, :, cols]
        for r in range(1, SUBLANE):
            slab[r, :, cols] = pltpu.roll(base, rows - r, 0)

    off = CONV_HALO - CONV_PAD
    for rc in range(ts // CONV_RC):
        base = rc * CONV_RC
        acc = jnp.zeros((CONV_RC, CONV_DIM), F32)
        for k in range(CONV_WIDTH):
            r = (off + k) % SUBLANE
            wk = jnp.concatenate([cw[k]] * (CONV_RC // SUBLANE), axis=0)
            acc = acc + slab[r, pl.ds(base + off + k - r, CONV_RC), :] * wk
        dw = acc + cb[...]
        mu = jnp.mean(dw, axis=-1, keepdims=True)
        xc = dw - mu
        var = jnp.mean(xc * xc, axis=-1, keepdims=True)
        y = xc * lax.rsqrt(var + LN_EPS) * lg[...] + lb[...]
        res = _silu(y) * _silu(gate[pl.ds(base, CONV_RC), :].astype(F32))
        o_ref[pl.ds(base, CONV_RC), :] = res.astype(o_ref.dtype)


def _conv_branch(h, conv_w, conv_b, ln_g, ln_b, seq, ts=256):
    t = h.shape[0]
    nb = ts // CONV_HALO
    last_blk = t // CONV_HALO - 1
    c1, c2, cg = E_CONV_IN // CONV_DIM, E_CONV_IN // CONV_DIM + 1, E_CONV_GATE // CONV_DIM
    body = functools.partial(_conv_body, ts=ts, tiles_per_seq=seq // ts)
    vec = pl.BlockSpec((1, CONV_DIM), lambda i: (0, 0))
    return pl.pallas_call(
        body,
        grid=(t // ts,),
        in_specs=[
            pl.BlockSpec((ts, CONV_DIM), lambda i: (i, c1)),
            pl.BlockSpec((ts, CONV_DIM), lambda i: (i, c2)),
            pl.BlockSpec((CONV_HALO, CONV_DIM), lambda i: (jnp.maximum(i * nb - 1, 0), c1)),
            pl.BlockSpec((CONV_HALO, CONV_DIM), lambda i: (jnp.maximum(i * nb - 1, 0), c2)),
            pl.BlockSpec((CONV_HALO, CONV_DIM), lambda i: (jnp.minimum((i + 1) * nb, last_blk), c1)),
            pl.BlockSpec((CONV_HALO, CONV_DIM), lambda i: (jnp.minimum((i + 1) * nb, last_blk), c2)),
            pl.BlockSpec((ts, CONV_DIM), lambda i: (i, cg)),
            pl.BlockSpec((CONV_WIDTH, SUBLANE, CONV_DIM), lambda i: (0, 0, 0)),
            vec, vec, vec,
        ],
        out_specs=pl.BlockSpec((ts, CONV_DIM), lambda i: (i, 0)),
        out_shape=jax.ShapeDtypeStruct((t, CONV_DIM), BF16),
        scratch_shapes=[pltpu.VMEM((SUBLANE, ts + 2 * CONV_HALO, CONV_DIM), F32)],
        compiler_params=_cparams(("parallel",)),
        name="conv_branch",
    )(h, h, h, h, h, h, h, conv_w, conv_b, ln_g, ln_b)


def _rms(x, g):
    ms = jnp.mean(x * x, axis=-1, keepdims=True)
    return x * lax.rsqrt(ms + RMS_EPS) * g


MLA_PROJ_CHUNKS = 2


def _rope(x, c, s):
    half = QK_ROPE // 2
    lane = lax.broadcasted_iota(jnp.int32, x.shape, 1)
    swapped = jnp.where((lane & (QK_ROPE - 1)) < half,
                        pltpu.roll(x, LANE - half, 1),
                        pltpu.roll(x, half, 1))
    return x * c + swapped * s


def _mla_proj_body(cq, ckv, kr, cos, sin, qn_g, kvn_g, wq, wkv, q_ref, k_ref, v_ref, *, scale):
    nh = MLA_HEADS * QK_NOPE
    nr = MLA_HEADS * QK_ROPE
    rc = cq.shape[0] // MLA_PROJ_CHUNKS
    for r0 in range(0, cq.shape[0], rc):
        rows = pl.ds(r0, rc)
        c = cos[0, rows, :]
        s = sin[0, rows, :]
        nq = _rms(cq[rows, :].astype(F32), qn_g[...]).astype(BF16)
        qf = jnp.dot(nq, wq[...], preferred_element_type=F32)
        qr = [_rope(qf[:, nh + p * LANE:nh + (p + 1) * LANE], c, s) for p in range(nr // LANE)]
        nkv = _rms(ckv[rows, :].astype(F32), kvn_g[...]).astype(BF16)
        kvf = jnp.dot(nkv, wkv[...], preferred_element_type=F32)
        kr_even = _rope(kr[rows, :].astype(F32), c, s)
        kr_odd = pltpu.roll(kr_even, QK_ROPE, 1)
        for h in range(MLA_HEADS):
            q_ref[0, h, rows, 0:QK_NOPE] = (qf[:, h * QK_NOPE:(h + 1) * QK_NOPE] * scale).astype(BF16)
            q_ref[0, h, rows, QK_NOPE:2 * QK_NOPE] = (qr[h // 2] * scale).astype(BF16)
            k_ref[0, h, rows, 0:QK_NOPE] = kvf[:, h * QK_NOPE:(h + 1) * QK_NOPE].astype(BF16)
            k_ref[0, h, rows, QK_NOPE:2 * QK_NOPE] = (kr_even if h % 2 == 0 else kr_odd).astype(BF16)
            v_ref[0, h, rows, :] = kvf[:, nh + h * V_DIM:nh + (h + 1) * V_DIM].astype(BF16)


def _mla_proj(h, cos128, sin128, q_norm, kv_norm, wq, wkv, bsz, seq, tm=1024):
    nt = seq // tm
    scale = (QK_NOPE + QK_ROPE) ** -0.5 * math.log2(math.e)
    body = functools.partial(_mla_proj_body, scale=scale)
    hd = 2 * QK_NOPE
    return pl.pallas_call(
        body,
        grid=(bsz, nt),
        in_specs=[
            pl.BlockSpec((tm, Q_LORA), lambda b, i: (b * nt + i, E_CQ // Q_LORA)),
            pl.BlockSpec((tm, KV_LORA), lambda b, i: (b * nt + i, E_CKV // KV_LORA)),
            pl.BlockSpec((tm, LANE), lambda b, i: (b * nt + i, E_KROPE // LANE)),
            pl.BlockSpec((1, tm, LANE), lambda b, i: (b, i, 0)),
            pl.BlockSpec((1, tm, LANE), lambda b, i: (b, i, 0)),
            pl.BlockSpec((1, Q_LORA), lambda b, i: (0, 0)),
            pl.BlockSpec((1, KV_LORA), lambda b, i: (0, 0)),
            pl.BlockSpec(wq.shape, lambda b, i: (0, 0)),
            pl.BlockSpec(wkv.shape, lambda b, i: (0, 0)),
        ],
        out_specs=[
            pl.BlockSpec((1, MLA_HEADS, tm, hd), lambda b, i: (b, 0, i, 0)),
            pl.BlockSpec((1, MLA_HEADS, tm, hd), lambda b, i: (b, 0, i, 0)),
            pl.BlockSpec((1, MLA_HEADS, tm, V_DIM), lambda b, i: (b, 0, i, 0)),
        ],
        out_shape=[
            jax.ShapeDtypeStruct((bsz, MLA_HEADS, seq, hd), BF16),
            jax.ShapeDtypeStruct((bsz, MLA_HEADS, seq, hd), BF16),
            jax.ShapeDtypeStruct((bsz, MLA_HEADS, seq, V_DIM), BF16),
        ],
        compiler_params=_cparams(("parallel", "parallel")),
        name="mla_proj",
    )(h, h, h, cos128, sin128, q_norm, kv_norm, wq, wkv)


def _flash_body(q_ref, k_ref, v_ref, g_ref, o_ref, *, tk, nk, unroll):
    q = q_ref[0, 0]
    tq = q.shape[0]

    ones = jnp.ones((tk, V_DIM), BF16)

    def step(c, carry):
        m, acc = carry
        start = pl.multiple_of(c * tk, tk)
        ks = k_ref[0, 0, pl.ds(start, tk), :]
        vs = jnp.concatenate([v_ref[0, 0, pl.ds(start, tk), :], ones], axis=1)
        s = lax.dot_general(q, ks, (((1,), (1,)), ((), ())), preferred_element_type=F32)
        m_new = jnp.maximum(m, jnp.max(s, axis=-1, keepdims=True))
        alpha = jnp.exp2(m - m_new)
        p = jnp.exp2(s - m_new)
        acc = alpha * acc + jnp.dot(p.astype(BF16), vs, preferred_element_type=F32)
        return m_new, acc

    m0 = jnp.full((tq, 1), -jnp.inf, F32)
    a0 = jnp.zeros((tq, 2 * V_DIM), F32)
    m, acc = lax.fori_loop(0, nk, step, (m0, a0), unroll=unroll)
    o_ref[...] = (acc[:, :V_DIM] / acc[:, V_DIM:] * _silu(g_ref[...].astype(F32))).astype(o_ref.dtype)


def _mla_attention(q, k, v, h, tq=2048, tk=256, unroll=True):
    bsz, heads, seq, hd = q.shape
    nq = seq // tq
    body = functools.partial(_flash_body, tk=tk, nk=seq // tk, unroll=unroll)
    gcol = E_MLA_GATE // V_DIM
    return pl.pallas_call(
        body,
        grid=(bsz, heads, nq),
        in_specs=[
            pl.BlockSpec((1, 1, tq, hd), lambda b, hh, i: (b, hh, i, 0)),
            pl.BlockSpec((1, 1, seq, hd), lambda b, hh, i: (b, hh, 0, 0)),
            pl.BlockSpec((1, 1, seq, V_DIM), lambda b, hh, i: (b, hh, 0, 0)),
            pl.BlockSpec((tq, V_DIM), lambda b, hh, i: (b * nq + i, gcol + hh)),
        ],
        out_specs=pl.BlockSpec((tq, V_DIM), lambda b, hh, i: (b * nq + i, hh)),
        out_shape=jax.ShapeDtypeStruct((bsz * seq, heads * V_DIM), BF16),
        compiler_params=_cparams(("parallel", "parallel", "arbitrary")),
        name="mla_attention",
    )(q, k, v, h)


def _mem_attn_body(q_ref, g_ref, kv_ref, o_ref, *, scale):
    ones = jnp.ones((kv_ref.shape[1], MEM_HEAD_DIM), BF16)
    for hh in range(MEM_HEADS):
        lo, hi = hh * MEM_HEAD_DIM, (hh + 1) * MEM_HEAD_DIM
        kk = kv_ref[0, :, lo:hi]
        vv = jnp.concatenate([kv_ref[0, :, MEM_DIM + lo:MEM_DIM + hi], ones], axis=1)
        s = lax.dot_general(q_ref[:, lo:hi], kk, (((1,), (1,)), ((), ())),
                            preferred_element_type=F32) * scale
        m = jnp.max(s, axis=-1, keepdims=True)
        p = jnp.exp2(s - m)
        o = jnp.dot(p.astype(BF16), vv, preferred_element_type=F32)
        o = o[:, :MEM_HEAD_DIM] / o[:, MEM_HEAD_DIM:]
        o_ref[:, lo:hi] = (o * _silu(g_ref[:, lo:hi].astype(F32))).astype(o_ref.dtype)


def _mem_attention(h, memkv, q_off, g_off, seq, tm=1024):
    t = h.shape[0]
    nt = seq // tm
    mlen = memkv.shape[1]
    body = functools.partial(_mem_attn_body, scale=MEM_HEAD_DIM ** -0.5 * math.log2(math.e))
    return pl.pallas_call(
        body,
        grid=(t // tm,),
        in_specs=[
            pl.BlockSpec((tm, MEM_DIM), lambda i: (i, q_off // MEM_DIM)),
            pl.BlockSpec((tm, MEM_DIM), lambda i: (i, g_off // MEM_DIM)),
            pl.BlockSpec((1, mlen, 2 * MEM_DIM), lambda i: (i // nt, 0, 0)),
        ],
        out_specs=pl.BlockSpec((tm, MEM_DIM), lambda i: (i, 0)),
        out_shape=jax.ShapeDtypeStruct((t, MEM_DIM), BF16),
        compiler_params=_cparams(("parallel",)),
        name="mem_attention",
    )(h, h, memkv)


OUT_LN_CHUNKS = 4


def _out_ln_body(*refs, nparts):
    parts = refs[:nparts]
    w_ref, x_ref, g_ref, b_ref, o_ref = refs[nparts:]
    rc = o_ref.shape[0] // OUT_LN_CHUNKS
    for r0 in range(0, o_ref.shape[0], rc):
        rows = pl.ds(r0, rc)
        cat = jnp.concatenate([p[rows, :] for p in parts], axis=1)
        y = jnp.dot(cat, w_ref[...], preferred_element_type=F32)
        z = ALPHA * x_ref[rows, :] + y
        mu = jnp.mean(z, axis=-1, keepdims=True)
        zc = z - mu
        var = jnp.mean(zc * zc, axis=-1, keepdims=True)
        o_ref[rows, :] = zc * lax.rsqrt(var + LN_EPS) * g_ref[...] + b_ref[...]


def _out_ln(parts, w, x, g, b, tm=1024, name="out_ln"):
    t = x.shape[0]
    n = len(parts)
    body = functools.partial(_out_ln_body, nparts=n)
    assert sum(p.shape[1] for p in parts) == w.shape[0]
    in_specs = [pl.BlockSpec((tm, p.shape[1]), lambda i: (i, 0)) for p in parts]
    in_specs += [pl.BlockSpec(w.shape, lambda i: (0, 0)),
                 pl.BlockSpec((tm, D_MODEL), lambda i: (i, 0)),
                 pl.BlockSpec((1, D_MODEL), lambda i: (0, 0)),
                 pl.BlockSpec((1, D_MODEL), lambda i: (0, 0))]
    return pl.pallas_call(
        body,
        grid=(t // tm,),
        in_specs=in_specs,
        out_specs=pl.BlockSpec((tm, D_MODEL), lambda i: (i, 0)),
        out_shape=jax.ShapeDtypeStruct((t, D_MODEL), F32),
        compiler_params=_cparams(("parallel",)),
        name=name,
    )(*parts, w, x, g, b)


S5_C2 = 8
PW_A8, PW_A64, PW_A512, PW_A1024, PW_A2048 = 0, 1, 2, 3, 4
PW_A64K = 5
PW_A8K = 13
PW_ROWS = 24
S5_PAD = 8


def _cmul(pr, pi, xr, xi):
    return pr * xr - pi * xi, pr * xi + pi * xr


def _s5_scan(s_ref, x_ref, pw_ref, rev):
    half = 512
    n0 = 8 * S5_C2
    n1 = S5_C2

    def pw(row):
        v = pw_ref[pl.ds(row, 1), :]
        return v[:, :half], v[:, half:]

    def blk0(k):
        return (7 - k if rev else k) * n0

    def blk1(k):
        return (7 - k if rev else k) * n1

    def ld(ref, start, n):
        v = ref[pl.ds(start, n), :]
        return v[:, :half], v[:, half:]

    def st(ref, start, n, re, im):
        ref[pl.ds(start, n), 0:half] = re
        ref[pl.ds(start, n), half:2 * half] = im

    a8r, a8i = pw(PW_A8)
    for k in range(1, 8):
        pr_, pi_ = ld(s_ref, blk0(k - 1), n0)
        cr, ci = ld(s_ref, blk0(k), n0)
        mr, mi = _cmul(a8r, a8i, pr_, pi_)
        st(s_ref, blk0(k), n0, cr + mr, ci + mi)
    g0 = blk0(7)
    a64r, a64i = pw(PW_A64)
    for k in range(1, 8):
        pr_, pi_ = ld(s_ref, g0 + blk1(k - 1), n1)
        cr, ci = ld(s_ref, g0 + blk1(k), n1)
        mr, mi = _cmul(a64r, a64i, pr_, pi_)
        st(s_ref, g0 + blk1(k), n1, cr + mr, ci + mi)
    hr, hi = ld(s_ref, g0 + blk1(7), n1)
    c2 = lax.broadcasted_iota(jnp.int32, (n1, half), 0)
    if rev:
        c2 = (S5_C2 - 1) - c2
    for d, row in ((1, PW_A512), (2, PW_A1024), (4, PW_A2048)):
        ar, ai = pw(row)
        sh = (n1 - d) if rev else d
        sr = pltpu.roll(hr, sh, 0)
        si = pltpu.roll(hi, sh, 0)
        mr, mi = _cmul(ar, ai, sr, si)
        keep = c2 >= d
        hr = hr + jnp.where(keep, mr, 0.0)
        hi = hi + jnp.where(keep, mi, 0.0)
    sh1 = (n1 - 1) if rev else 1
    p3r = jnp.where(c2 >= 1, pltpu.roll(hr, sh1, 0), 0.0)
    p3i = jnp.where(c2 >= 1, pltpu.roll(hi, sh1, 0), 0.0)
    for k1 in range(8):
        if k1 == 0:
            er, ei = p3r, p3i
        else:
            ar, ai = pw(PW_A64K + k1 - 1)
            mr, mi = _cmul(ar, ai, p3r, p3i)
            qr, qi = ld(s_ref, g0 + blk1(k1 - 1), n1)
            er, ei = qr + mr, qi + mi
        for k0 in range(8):
            dst = blk0(k0) + blk1(k1)
            if k0 == 0:
                st(x_ref, dst, n1, er, ei)
            else:
                ar, ai = pw(PW_A8K + k0)
                mr, mi = _cmul(ar, ai, er, ei)
                qr, qi = ld(s_ref, blk0(k0 - 1) + blk1(k1), n1)
                st(x_ref, dst, n1, qr + mr, qi + mi)


def _lane_select(d_log2, tt):
    kk = lax.broadcasted_iota(jnp.int32, (LANE, LANE), 0)
    col = tt * LANE + lax.broadcasted_iota(jnp.int32, (LANE, LANE), 1)
    src = lax.shift_left(lax.shift_right_logical(col, d_log2 + 3), d_log2) + (col & ((1 << d_log2) - 1))
    return jnp.where(kk == src, 1.0, 0.0).astype(BF16)


def _expand_blockdiag(comp, m_ref, row0, sels, *, row_b_log2, d_log2):
    rows = comp.shape[0]
    rowg = lax.shift_right_logical(lax.broadcasted_iota(jnp.int32, (rows, LANE), 0), row_b_log2) & 7
    coln = lax.broadcasted_iota(jnp.int32, (rows, LANE), 1)
    for tt in range(8):
        piece = jnp.dot(comp, sels[tt], preferred_element_type=F32)
        colg = lax.shift_right_logical(tt * LANE + coln, d_log2) & 7
        m_ref[pl.ds(row0, rows), tt * LANE:(tt + 1) * LANE] = jnp.where(
            rowg == colg, piece, 0.0).astype(BF16)


_NT = (((1,), (1,)), ((), ()))


def _expand_toeplitz(bx_ref, cax_ref, tw):
    rowg = lax.shift_right_logical(lax.broadcasted_iota(jnp.int32, (LANE, LANE), 0), 4)
    colg = lax.shift_right_logical(lax.broadcasted_iota(jnp.int32, (LANE, LANE), 1), 4)
    keep = rowg == colg

    def kern(n, d):
        return lax.dot_general(bx_ref[d, 0], cax_ref[n, d, 0], _NT,
                               precision=lax.Precision.HIGHEST, preferred_element_type=F32)

    for lag in range(-(S5_L - 1), S5_L):
        if lag == 0:
            k = kern(0, 0) + kern(0, 1)
        else:
            k = kern(lag, 0) if lag > 0 else kern(-lag, 1)
        blk = jnp.where(keep, k, 0.0).astype(BF16)
        for l in range(S5_L):
            lp = l + lag
            if 0 <= lp < S5_L:
                tw[l * LANE:(l + 1) * LANE, lp * LANE:(lp + 1) * LANE] = blk


def _expand_readout(cax_ref, v_ref, d, lags):
    half = 8 * S5_STATE
    row = lax.broadcasted_iota(jnp.int32, (half, LANE), 0)
    lane = lax.broadcasted_iota(jnp.int32, (half, LANE), 1)
    rowg = lax.shift_right_logical(row, 6)
    colg = lax.shift_right_logical(lane, 4)
    for r in range(2):
        pick = jnp.where(((row & (S5_STATE - 1)) + r * S5_STATE) == lane, 1.0, 0.0).astype(BF16)
        for lp, n in enumerate(lags):
            piece = lax.dot_general(pick, cax_ref[n, d, 0].astype(BF16), _NT, preferred_element_type=F32)
            v_ref[r * half:(r + 1) * half, lp * LANE:(lp + 1) * LANE] = jnp.where(
                rowg == colg, piece, 0.0).astype(BF16)


def _s5_body(u_ref, bx_ref, cax_ref, cw_ref, pwf_ref, pwb_ref, d_ref, o_ref,
             tw, wf, wb, vf, vb, stage, xcat, sbuf, xf, xb, *, seq):
    n1 = S5_C2
    blk_rows = seq // S5_C2
    pitch = blk_rows + S5_PAD

    @pl.when(pl.program_id(1) == 0)
    def _():
        _expand_toeplitz(bx_ref, cax_ref, tw)
        sel_w = [_lane_select(6, tt) for tt in range(8)]
        for l in range(S5_L):
            _expand_blockdiag(cw_ref[l, 0, 0], wf, l * LANE, sel_w, row_b_log2=4, d_log2=6)
            _expand_blockdiag(cw_ref[l, 1, 0], wb, l * LANE, sel_w, row_b_log2=4, d_log2=6)
        _expand_readout(cax_ref, vf, 0, [lp + 1 for lp in range(S5_L)])
        _expand_readout(cax_ref, vb, 1, [S5_L - lp for lp in range(S5_L)])

    def pieces():
        for c0 in range(8):
            for c1 in range(8):
                for l in range(S5_L):
                    yield (c0 * 8 * n1 + c1 * n1, l * LANE,
                           pl.ds(c1 * 64 + c0 * 8 + l, n1, stride=pitch))

    for c2 in range(S5_C2):
        stage[pl.ds(c2 * pitch, blk_rows), :] = u_ref[pl.ds(c2 * blk_rows, blk_rows), :]
    for r0, l0, tok in pieces():
        xcat[pl.ds(r0, n1), l0:l0 + LANE] = stage[tok, :]
    xb16 = xcat[...].astype(BF16)
    sbuf[...] = jnp.dot(xb16, wf[...], preferred_element_type=F32)
    _s5_scan(sbuf, xf, pwf_ref, False)
    sbuf[...] = jnp.dot(xb16, wb[...], preferred_element_type=F32)
    _s5_scan(sbuf, xb, pwb_ref, True)
    y = jnp.dot(xb16, tw[...], preferred_element_type=F32)
    y = y + jnp.dot(xf[...].astype(BF16), vf[...], preferred_element_type=F32)
    y = y + jnp.dot(xb[...].astype(BF16), vb[...], preferred_element_type=F32)
    dd = jnp.concatenate([d_ref[...]] * S5_L, axis=-1)
    sbuf[...] = jax.nn.gelu(y + dd * xcat[...])
    for r0, l0, tok in pieces():
        stage[tok, :] = sbuf[pl.ds(r0, n1), l0:l0 + LANE]
    for c2 in range(S5_C2):
        o_ref[pl.ds(c2 * blk_rows, blk_rows), :] = stage[pl.ds(c2 * pitch, blk_rows), :].astype(o_ref.dtype)


def _s5_branch(u, ops, d, bsz, seq):
    t = u.shape[0]
    bx, cax, cw, pw = ops
    ntile = S5_DIM // LANE
    rows = seq // S5_L
    wide = S5_L * LANE
    body = functools.partial(_s5_body, seq=seq)

    def pw_spec(di):
        return pl.BlockSpec((PW_ROWS, wide), lambda j, b: (0, di * ntile + j))

    return pl.pallas_call(
        body,
        grid=(ntile, bsz),
        in_specs=[pl.BlockSpec((seq, LANE), lambda j, b: (b, j)),
                  pl.BlockSpec((2, 1, LANE, LANE), lambda j, b: (0, j, 0, 0)),
                  pl.BlockSpec((S5_L + 1, 2, 1, LANE, LANE), lambda j, b: (0, 0, j, 0, 0)),
                  pl.BlockSpec((S5_L, 2, 1, LANE, LANE), lambda j, b: (0, 0, j, 0, 0)),
                  pw_spec(0), pw_spec(1),
                  pl.BlockSpec((1, LANE), lambda j, b: (0, j))],
        out_specs=pl.BlockSpec((seq, LANE), lambda j, b: (b, j)),
        out_shape=jax.ShapeDtypeStruct((t, S5_DIM), BF16),
        scratch_shapes=([pltpu.VMEM((wide, wide), BF16)] * 5
                        + [pltpu.VMEM((seq + S5_C2 * S5_PAD, LANE), F32)]
                        + [pltpu.VMEM((rows, wide), F32)] * 4),
        compiler_params=_cparams(("parallel", "arbitrary")),
        name="s5_branch",
    )(u, bx, cax, cw, pw, pw, d)


def _s5_prep(fwd, bwd):
    G, P, C, L = S5_GROUPS, S5_STATE, S5_GROUP, S5_L
    nt = G // 8
    a_re, a_im, log_dt, b_re, b_im, c_re, c_im = (jnp.stack([f, b]) for f, b in zip(fwd, bwd))
    lam_re = jnp.minimum(a_re, -1e-4)
    lam_im = a_im
    dt = jnp.exp(log_dt)[..., None]
    mag = jnp.exp(lam_re * dt)
    lb_re = mag * jnp.cos(lam_im * dt)
    lb_im = mag * jnp.sin(lam_im * dt)
    den = lam_re * lam_re + lam_im * lam_im
    nr = lb_re - 1.0
    f_re = (nr * lam_re + lb_im * lam_im) / den
    f_im = (lb_im * lam_re - nr * lam_im) / den
    bt_re = jnp.swapaxes(b_re, -1, -2)
    bt_im = jnp.swapaxes(b_im, -1, -2)
    bbt_re = f_re[:, :, None, :] * bt_re - f_im[:, :, None, :] * bt_im
    bbt_im = f_re[:, :, None, :] * bt_im + f_im[:, :, None, :] * bt_re

    def power(n):
        n = jnp.asarray(n, F32).reshape(-1, 1, 1, 1)
        m = jnp.exp(n * (lam_re * dt))
        ang = n * (lam_im * dt)
        return m * jnp.cos(ang), m * jnp.sin(ang)

    pr, pi = power(jnp.arange(L + 1))

    pr2 = jnp.concatenate([pr, -pi], axis=-1)[:, :, :, None, :]
    pi2 = jnp.concatenate([-pi, -pr], axis=-1)[:, :, :, None, :]
    c_ri = jnp.concatenate([c_re, c_re], axis=-1)[None]
    c_ii = jnp.concatenate([c_im, c_im], axis=-1)[None]
    cax = (c_ri * pr2 + c_ii * pi2).reshape(L + 1, 2, nt, 8 * C, 2 * P)
    bx = jnp.concatenate([bbt_re, bbt_im], axis=-1)

    prw = jnp.stack([pr[L - 1::-1, 0], pr[:L, 1]], axis=1)
    piw = jnp.stack([pi[L - 1::-1, 0], pi[:L, 1]], axis=1)
    aa = jnp.concatenate([prw, prw], axis=-1)[:, :, :, None, :]
    ab = jnp.concatenate([-piw, piw], axis=-1)[:, :, :, None, :]
    x2 = jnp.concatenate([bbt_im, bbt_re], axis=-1)[None]
    cw = (aa * bx[None] + ab * x2).astype(BF16).reshape(L, 2, nt, 8 * C, 2 * P)
    bx = bx.reshape(2, nt, 8 * C, 2 * P)

    ns = ([8.0, 64.0, 512.0, 1024.0, 2048.0] + [64.0 * (k + 1) for k in range(8)]
          + [8.0 * k for k in range(8)] + [0.0] * (PW_ROWS - 21))
    qr, qi = power(ns)
    pw = jnp.concatenate([qr.reshape(PW_ROWS, 2, nt, 8 * P), qi.reshape(PW_ROWS, 2, nt, 8 * P)],
                         axis=-1).reshape(PW_ROWS, 2 * nt * 2 * 8 * P)
    return bx, cax, cw, pw


GLU_COLS = 256


def _glu_mm_body(a_ref, w_ref, g_ref, o_ref):
    a = a_ref[...]
    n = o_ref.shape[1]
    for c0 in range(0, n, GLU_COLS):
        z1 = jnp.dot(a, w_ref[:, c0:c0 + GLU_COLS], preferred_element_type=F32)
        z2 = jnp.dot(a, w_ref[:, n + c0:n + c0 + GLU_COLS], preferred_element_type=F32)
        gate = _silu(g_ref[:, c0:c0 + GLU_COLS].astype(F32))
        o_ref[:, c0:c0 + GLU_COLS] = (z1 * jax.nn.sigmoid(z2) * gate).astype(o_ref.dtype)


def _glu_matmul(a, w, h2, tm=1024):
    t, k = a.shape
    n = w.shape[1] // 2
    return pl.pallas_call(
        _glu_mm_body,
        grid=(t // tm,),
        in_specs=[pl.BlockSpec((tm, k), lambda i: (i, 0)),
                  pl.BlockSpec((k, 2 * n), lambda i: (0, 0)),
                  pl.BlockSpec((tm, n), lambda i: (i, O_GATE // n))],
        out_specs=pl.BlockSpec((tm, n), lambda i: (i, 0)),
        out_shape=jax.ShapeDtypeStruct((t, n), BF16),
        compiler_params=_cparams(("parallel",)),
        name="s5_glu",
    )(a, w, h2)


def _even_in_weight(w_in):
    k = w_in.shape[0]
    conv_in, conv_gate, c_q, c_kv, k_rope, mla_gate, mem_q, mem_gate = jnp.split(
        w_in, [2048, 3072, 3840, 4096, 4160, 5184, 5696], axis=1)
    z = jnp.zeros((k, LANE - QK_ROPE), w_in.dtype)
    return jnp.concatenate([conv_in, conv_gate, c_q, c_kv, mla_gate, mem_q, mem_gate, k_rope, z],
                           axis=1).astype(BF16)


def _uq_weight(w_uq):
    k = w_uq.shape[0]
    w = w_uq.reshape(k, MLA_HEADS, QK_NOPE + QK_ROPE)
    nope = w[:, :, :QK_NOPE].reshape(k, MLA_HEADS * QK_NOPE)
    rope = w[:, :, QK_NOPE:].reshape(k, MLA_HEADS * QK_ROPE)
    return jnp.concatenate([nope, rope], axis=1).astype(BF16)


def _ukv_weight(w_ukv):
    k = w_ukv.shape[0]
    w = w_ukv.reshape(k, MLA_HEADS, QK_NOPE + V_DIM)
    kn = w[:, :, :QK_NOPE].reshape(k, MLA_HEADS * QK_NOPE)
    vv = w[:, :, QK_NOPE:].reshape(k, MLA_HEADS * V_DIM)
    return jnp.concatenate([kn, vv], axis=1).astype(BF16)


def _rope_tables(positions):
    inv_freq = ROPE_THETA ** (-jnp.arange(0, QK_ROPE, 2, dtype=F32) / QK_ROPE)
    half = QK_ROPE // 2
    inv4 = jnp.tile(inv_freq, LANE // half)
    sign = jnp.tile(jnp.concatenate([-jnp.ones((half,), F32), jnp.ones((half,), F32)]),
                    LANE // QK_ROPE)
    ang = positions.astype(F32)[..., None] * inv4
    return jnp.cos(ang), jnp.sin(ang) * sign


def _row(v):
    return v.reshape(1, -1)


def _even_layer(x2, mem2, cos128, sin128, bsz, seq, w_in, conv_w, conv_b, conv_ln_g, conv_ln_b,
                q_norm, w_uq, kv_norm, w_ukv, w_mem_kv, w_out, ln_g, ln_b):
    (h,) = _proj(x2, _even_in_weight(w_in), [(E_NPAD, BF16)], 512, "even_in_proj")
    taps = jnp.broadcast_to(conv_w.reshape(CONV_WIDTH, 1, CONV_DIM), (CONV_WIDTH, SUBLANE, CONV_DIM))
    a_out = _conv_branch(h, taps, _row(conv_b), _row(conv_ln_g),
                         _row(conv_ln_b), seq)
    q, k, v = _mla_proj(h, cos128, sin128, _row(q_norm), _row(kv_norm), _uq_weight(w_uq),
                        _ukv_weight(w_ukv), bsz, seq)
    b_out = _mla_attention(q, k, v, h)
    (memkv,) = _proj(mem2, w_mem_kv.astype(BF16), [(2 * MEM_DIM, BF16)], 512, "even_mem_kv")
    memkv = memkv.reshape(bsz, -1, 2 * MEM_DIM)
    m_out = _mem_attention(h, memkv, E_MEM_Q, E_MEM_GATE, seq)
    return _out_ln([a_out, b_out, m_out], w_out.astype(BF16), x2, _row(ln_g), _row(ln_b),
                   name="even_out_ln")


def _odd_layer(x2, mem2, bsz, seq, w_in, s5_fwd, s5_bwd, s5_d, w_glu, w_mem_kv, w_out, ln_g, ln_b):
    n_in = w_in.shape[1]
    u, h2 = _proj(x2, w_in.astype(BF16), [(O_SPLIT, F32), (n_in - O_SPLIT, BF16)], 1024,
                  "odd_in_proj")
    g = _s5_branch(u, _s5_prep(s5_fwd, s5_bwd), _row(s5_d), bsz, seq)
    c_out = _glu_matmul(g, w_glu.astype(BF16), h2)
    (memkv,) = _proj(mem2, w_mem_kv.astype(BF16), [(2 * MEM_DIM, BF16)], 512, "odd_mem_kv")
    memkv = memkv.reshape(bsz, -1, 2 * MEM_DIM)
    m_out = _mem_attention(h2, memkv, O_MEM_Q, O_MEM_GATE, seq)
    return _out_ln([c_out, m_out], w_out.astype(BF16), x2, _row(ln_g), _row(ln_b),
                   name="odd_out_ln")


def kernel(x, mem, positions, e_w_in, e_conv_w, e_conv_b, e_conv_ln_g, e_conv_ln_b, e_q_norm, e_w_uq, e_kv_norm, e_w_ukv, e_mem_kv, e_w_out, e_ln_g, e_ln_b, o_w_in, o_a_re_f, o_a_im_f, o_log_dt_f, o_b_re_f, o_b_im_f, o_c_re_f, o_c_im_f, o_a_re_b, o_a_im_b, o_log_dt_b, o_b_re_b, o_b_im_b, o_c_re_b, o_c_im_b, o_d, o_w_glu, o_mem_kv, o_w_out, o_ln_g, o_ln_b):
    bsz, seq, d = x.shape
    cos128, sin128 = _rope_tables(positions)
    x2 = x.reshape(bsz * seq, d)
    mem2 = mem.reshape(-1, d)
    h = _even_layer(x2, mem2, cos128, sin128, bsz, seq, e_w_in[0], e_conv_w[0], e_conv_b[0],
                    e_conv_ln_g[0], e_conv_ln_b[0], e_q_norm[0], e_w_uq[0], e_kv_norm[0],
                    e_w_ukv[0], e_mem_kv[0], e_w_out[0], e_ln_g[0], e_ln_b[0])
    s5_fwd = (o_a_re_f[0], o_a_im_f[0], o_log_dt_f[0], o_b_re_f[0], o_b_im_f[0], o_c_re_f[0], o_c_im_f[0])
    s5_bwd = (o_a_re_b[0], o_a_im_b[0], o_log_dt_b[0], o_b_re_b[0], o_b_im_b[0], o_c_re_b[0], o_c_im_b[0])
    h = _odd_layer(h, mem2, bsz, seq, o_w_in[0], s5_fwd, s5_bwd, o_d[0], o_w_glu[0], o_mem_kv[0],
                   o_w_out[0], o_ln_g[0], o_ln_b[0])
    return h.reshape(bsz, seq, d)
```

```python
import functools
import math

import jax
import jax.numpy as jnp
from jax import lax
from jax.experimental import pallas as pl
from jax.experimental.pallas import tpu as pltpu

F32 = jnp.float32
BF16 = jnp.bfloat16

D_MODEL = 1024
CONV_DIM = 1024
CONV_WIDTH = 31
CONV_PAD = 15
MLA_HEADS = 8
QK_NOPE = 128
QK_ROPE = 64
V_DIM = 128
Q_LORA = 768
KV_LORA = 256
ROPE_THETA = 10000.0
MEM_HEADS = 4
MEM_HEAD_DIM = 128
MEM_DIM = 512
S5_DIM = 1024
S5_GROUP = 16
S5_GROUPS = 64
S5_STATE = 64
LN_EPS = 1e-5
RMS_EPS = 1e-6
DEPTH = 2
ALPHA = (2 * DEPTH) ** 0.25

LANE = 128
SUBLANE = 8
VMEM_LIMIT = 56 * 1024 * 1024

E_CONV = 3 * CONV_DIM
E_CQ = 0
E_CKV = 768
E_MLA_GATE = 1024
E_MEM_Q = 2048
E_MEM_GATE = 2560
E_KROPE = 3072
E_NREST = 3200
E_HEAD = E_CONV + Q_LORA + KV_LORA
O_SPLIT = 1024
O_GATE = 0
O_MEM_Q = 1024
O_MEM_GATE = 1536

S5_L = 8


def _cparams(sem):
    return pltpu.CompilerParams(dimension_semantics=sem, vmem_limit_bytes=VMEM_LIMIT)


def _silu(x):
    return x * jax.nn.sigmoid(x)


def _proj_body(a_ref, w_ref, *o_refs, bounds):
    y = jnp.dot(a_ref[...].astype(BF16), w_ref[...], preferred_element_type=F32)
    for o_ref, (lo, hi) in zip(o_refs, bounds):
        o_ref[...] = y[:, lo:hi].astype(o_ref.dtype)


def _proj(a, w, outs, tm, name):
    m, k = a.shape
    n = w.shape[1]
    tm = min(tm, m)
    assert m % tm == 0 and sum(width for width, _ in outs) == n
    bounds, lo = [], 0
    for width, _ in outs:
        bounds.append((lo, lo + width))
        lo += width
    res = pl.pallas_call(
        functools.partial(_proj_body, bounds=tuple(bounds)),
        grid=(m // tm,),
        in_specs=[pl.BlockSpec((tm, k), lambda i: (i, 0)),
                  pl.BlockSpec((k, n), lambda i: (0, 0))],
        out_specs=[pl.BlockSpec((tm, width), lambda i: (i, 0)) for width, _ in outs],
        out_shape=[jax.ShapeDtypeStruct((m, width), dt) for width, dt in outs],
        compiler_params=_cparams(("parallel",)),
        name=name,
    )(a, w)
    return res


CONV_HALO = 16
CONV_RC = 32


def _in_conv_body(x_ref, wa_ref, wb_ref, cw, cb, lg, lb, h_ref, o_ref, xb, slab, cur, gcur, gate, tail,
                  *, ts, tiles_per_seq):
    i = pl.program_id(0)

    @pl.when(i == 0)
    def _():
        cur[...] = jnp.zeros_like(cur)
        gcur[...] = jnp.zeros_like(gcur)
        tail[...] = jnp.zeros_like(tail)

    xb[...] = x_ref[...].astype(BF16)
    na = wa_ref.shape[1]
    n = na + wb_ref.shape[1]

    def proj(lo, hi):
        assert hi <= na or lo >= na
        w = wa_ref[:, lo:hi] if hi <= na else wb_ref[:, lo - na:hi - na]
        return jnp.dot(xb[...], w, preferred_element_type=F32)

    glu_new = proj(0, CONV_DIM) * jax.nn.sigmoid(proj(CONV_DIM, 2 * CONV_DIM))

    j = i - 1
    first = (j % tiles_per_seq) == 0
    last = (j % tiles_per_seq) == tiles_per_seq - 1
    slab[0, pl.ds(CONV_HALO, ts), :] = cur[...]
    slab[0, pl.ds(0, CONV_HALO), :] = jnp.where(first, 0.0, tail[...])
    slab[0, pl.ds(CONV_HALO + ts, CONV_HALO), :] = jnp.where(last, 0.0, glu_new[0:CONV_HALO, :])
    gate[...] = gcur[...]
    tail[...] = cur[pl.ds(ts - CONV_HALO, CONV_HALO), :]
    cur[...] = glu_new

    rows = ts + 2 * CONV_HALO
    for lt in range(CONV_DIM // LANE):
        cols = slice(lt * LANE, (lt + 1) * LANE)
        base = slab[0, :, cols]
        for r in range(1, SUBLANE):
            slab[r, :, cols] = pltpu.roll(base, rows - r, 0)

    nchunks = ts // CONV_RC
    step_cols = (n - 2 * CONV_DIM) // nchunks // LANE * LANE
    off = CONV_HALO - CONV_PAD
    for rc in range(nchunks):
        lo = 2 * CONV_DIM + rc * step_cols
        hi = n if rc == nchunks - 1 else lo + step_cols
        yp = proj(lo, hi)
        if hi <= E_CONV:
            gcur[:, lo - 2 * CONV_DIM:hi - 2 * CONV_DIM] = yp.astype(gcur.dtype)
        else:
            assert lo >= E_CONV
            h_ref[:, lo - E_CONV:hi - E_CONV] = yp.astype(h_ref.dtype)

        base = rc * CONV_RC
        acc = jnp.zeros((CONV_RC, CONV_DIM), F32)
        for k in range(CONV_WIDTH):
            r = (off + k) % SUBLANE
            wk = jnp.concatenate([cw[k]] * (CONV_RC // SUBLANE), axis=0)
            acc = acc + slab[r, pl.ds(base + off + k - r, CONV_RC), :] * wk
        dw = acc + cb[...]
        mu = jnp.mean(dw, axis=-1, keepdims=True)
        xc = dw - mu
        var = jnp.mean(xc * xc, axis=-1, keepdims=True)
        yn = xc * lax.rsqrt(var + LN_EPS) * lg[...] + lb[...]
        res = _silu(yn) * _silu(gate[pl.ds(base, CONV_RC), :].astype(F32))
        o_ref[pl.ds(base, CONV_RC), :] = res.astype(o_ref.dtype)


def _in_proj_conv(x, w_full, w_tail, taps, conv_b, ln_g, ln_b, seq, ts=256):
    t, k = x.shape
    n = E_HEAD + w_tail.shape[1]
    nt = t // ts
    body = functools.partial(_in_conv_body, ts=ts, tiles_per_seq=seq // ts)
    vec = pl.BlockSpec((1, CONV_DIM), lambda i: (0, 0))
    return pl.pallas_call(
        body,
        grid=(nt + 1,),
        in_specs=[
            pl.BlockSpec((ts, k), lambda i: (jnp.minimum(i, nt - 1), 0)),
            pl.BlockSpec((k, E_HEAD), lambda i: (0, 0)),
            pl.BlockSpec(w_tail.shape, lambda i: (0, 0)),
            pl.BlockSpec((CONV_WIDTH, SUBLANE, CONV_DIM), lambda i: (0, 0, 0)),
            vec, vec, vec,
        ],
        out_specs=[pl.BlockSpec((ts, n - E_CONV), lambda i: (jnp.minimum(i, nt - 1), 0)),
                   pl.BlockSpec((ts, CONV_DIM), lambda i: (jnp.maximum(i - 1, 0), 0))],
        out_shape=[jax.ShapeDtypeStruct((t, n - E_CONV), BF16),
                   jax.ShapeDtypeStruct((t, CONV_DIM), BF16)],
        scratch_shapes=[pltpu.VMEM((ts, k), BF16),
                        pltpu.VMEM((SUBLANE, ts + 2 * CONV_HALO, CONV_DIM), F32),
                        pltpu.VMEM((ts, CONV_DIM), F32),
                        pltpu.VMEM((ts, CONV_DIM), BF16),
                        pltpu.VMEM((ts, CONV_DIM), BF16),
                        pltpu.VMEM((CONV_HALO, CONV_DIM), F32)],
        compiler_params=_cparams(("arbitrary",)),
        name="in_proj_conv",
    )(x, w_full, w_tail, taps, conv_b, ln_g, ln_b)


def _rms(x, g):
    ms = jnp.mean(x * x, axis=-1, keepdims=True)
    return x * lax.rsqrt(ms + RMS_EPS) * g


MLA_PROJ_CHUNKS = 2


def _rope(x, c, s):
    half = QK_ROPE // 2
    lane = lax.broadcasted_iota(jnp.int32, x.shape, 1)
    swapped = jnp.where((lane & (QK_ROPE - 1)) < half,
                        pltpu.roll(x, LANE - half, 1),
                        pltpu.roll(x, half, 1))
    return x * c + swapped * s


def _mla_proj_body(cq, ckv, kr, cos, sin, qn_g, kvn_g, wq, wkv, q_ref, k_ref, v_ref, *, scale):
    nh = MLA_HEADS * QK_NOPE
    nr = MLA_HEADS * QK_ROPE
    rc = cq.shape[0] // MLA_PROJ_CHUNKS
    for r0 in range(0, cq.shape[0], rc):
        rows = pl.ds(r0, rc)
        c = cos[0, rows, :]
        s = sin[0, rows, :]
        nq = _rms(cq[rows, :].astype(F32), qn_g[...]).astype(BF16)
        qf = jnp.dot(nq, wq[...], preferred_element_type=F32)
        qr = [_rope(qf[:, nh + p * LANE:nh + (p + 1) * LANE], c, s) for p in range(nr // LANE)]
        nkv = _rms(ckv[rows, :].astype(F32), kvn_g[...]).astype(BF16)
        kvf = jnp.dot(nkv, wkv[...], preferred_element_type=F32)
        kr_even = _rope(kr[rows, :].astype(F32), c, s)
        kr_odd = pltpu.roll(kr_even, QK_ROPE, 1)
        for h in range(MLA_HEADS):
            q_ref[0, h, rows, 0:QK_NOPE] = (qf[:, h * QK_NOPE:(h + 1) * QK_NOPE] * scale).astype(BF16)
            q_ref[0, h, rows, QK_NOPE:2 * QK_NOPE] = (qr[h // 2] * scale).astype(BF16)
            k_ref[0, h, rows, 0:QK_NOPE] = kvf[:, h * QK_NOPE:(h + 1) * QK_NOPE].astype(BF16)
            k_ref[0, h, rows, QK_NOPE:2 * QK_NOPE] = (kr_even if h % 2 == 0 else kr_odd).astype(BF16)
            v_ref[0, h, rows, :] = kvf[:, nh + h * V_DIM:nh + (h + 1) * V_DIM].astype(BF16)


def _mla_proj(h, cos128, sin128, q_norm, kv_norm, wq, wkv, bsz, seq, tm=1024):
    nt = seq // tm
    scale = (QK_NOPE + QK_ROPE) ** -0.5 * math.log2(math.e)
    body = functools.partial(_mla_proj_body, scale=scale)
    hd = 2 * QK_NOPE
    return pl.pallas_call(
        body,
        grid=(bsz, nt),
        in_specs=[
            pl.BlockSpec((tm, Q_LORA), lambda b, i: (b * nt + i, E_CQ // Q_LORA)),
            pl.BlockSpec((tm, KV_LORA), lambda b, i: (b * nt + i, E_CKV // KV_LORA)),
            pl.BlockSpec((tm, LANE), lambda b, i: (b * nt + i, E_KROPE // LANE)),
            pl.BlockSpec((1, tm, LANE), lambda b, i: (b, i, 0)),
            pl.BlockSpec((1, tm, LANE), lambda b, i: (b, i, 0)),
            pl.BlockSpec((1, Q_LORA), lambda b, i: (0, 0)),
            pl.BlockSpec((1, KV_LORA), lambda b, i: (0, 0)),
            pl.BlockSpec(wq.shape, lambda b, i: (0, 0)),
            pl.BlockSpec(wkv.shape, lambda b, i: (0, 0)),
        ],
        out_specs=[
            pl.BlockSpec((1, MLA_HEADS, tm, hd), lambda b, i: (b, 0, i, 0)),
            pl.BlockSpec((1, MLA_HEADS, tm, hd), lambda b, i: (b, 0, i, 0)),
            pl.BlockSpec((1, MLA_HEADS, tm, V_DIM), lambda b, i: (b, 0, i, 0)),
        ],
        out_shape=[
            jax.ShapeDtypeStruct((bsz, MLA_HEADS, seq, hd), BF16),
            jax.ShapeDtypeStruct((bsz, MLA_HEADS, seq, hd), BF16),
            jax.ShapeDtypeStruct((bsz, MLA_HEADS, seq, V_DIM), BF16),
        ],
        compiler_params=_cparams(("parallel", "parallel")),
        name="mla_proj",
    )(h, h, h, cos128, sin128, q_norm, kv_norm, wq, wkv)


def _flash_body(q_ref, k_ref, v_ref, g_ref, o_ref, *, tk, nk, unroll):
    q = q_ref[0, 0]
    tq = q.shape[0]

    ones = jnp.ones((tk, V_DIM), BF16)

    def step(c, carry):
        m, acc = carry
        start = pl.multiple_of(c * tk, tk)
        ks = k_ref[0, 0, pl.ds(start, tk), :]
        vs = jnp.concatenate([v_ref[0, 0, pl.ds(start, tk), :], ones], axis=1)
        s = lax.dot_general(q, ks, (((1,), (1,)), ((), ())), preferred_element_type=F32)
        m_new = jnp.maximum(m, jnp.max(s, axis=-1, keepdims=True))
        alpha = jnp.exp2(m - m_new)
        p = jnp.exp2(s - m_new)
        acc = alpha * acc + jnp.dot(p.astype(BF16), vs, preferred_element_type=F32)
        return m_new, acc

    m0 = jnp.full((tq, 1), -jnp.inf, F32)
    a0 = jnp.zeros((tq, 2 * V_DIM), F32)
    m, acc = lax.fori_loop(0, nk, step, (m0, a0), unroll=unroll)
    o_ref[...] = (acc[:, :V_DIM] / acc[:, V_DIM:] * _silu(g_ref[...].astype(F32))).astype(o_ref.dtype)


def _mla_attention(q, k, v, h, tq=2048, tk=256, unroll=True):
    bsz, heads, seq, hd = q.shape
    nq = seq // tq
    body = functools.partial(_flash_body, tk=tk, nk=seq // tk, unroll=unroll)
    gcol = E_MLA_GATE // V_DIM
    return pl.pallas_call(
        body,
        grid=(bsz, heads, nq),
        in_specs=[
            pl.BlockSpec((1, 1, tq, hd), lambda b, hh, i: (b, hh, i, 0)),
            pl.BlockSpec((1, 1, seq, hd), lambda b, hh, i: (b, hh, 0, 0)),
            pl.BlockSpec((1, 1, seq, V_DIM), lambda b, hh, i: (b, hh, 0, 0)),
            pl.BlockSpec((tq, V_DIM), lambda b, hh, i: (b * nq + i, gcol + hh)),
        ],
        out_specs=pl.BlockSpec((tq, V_DIM), lambda b, hh, i: (b * nq + i, hh)),
        out_shape=jax.ShapeDtypeStruct((bsz * seq, heads * V_DIM), BF16),
        compiler_params=_cparams(("parallel", "parallel", "arbitrary")),
        name="mla_attention",
    )(q, k, v, h)


OUT_LN_CHUNKS = 4


def _mem_attn(q_ref, g_ref, kv_ref, rows, scale):
    ones = jnp.ones((kv_ref.shape[1], MEM_HEAD_DIM), BF16)
    outs = []
    for hh in range(MEM_HEADS):
        lo, hi = hh * MEM_HEAD_DIM, (hh + 1) * MEM_HEAD_DIM
        kk = kv_ref[0, :, lo:hi]
        vv = jnp.concatenate([kv_ref[0, :, MEM_DIM + lo:MEM_DIM + hi], ones], axis=1)
        s = lax.dot_general(q_ref[rows, lo:hi], kk, (((1,), (1,)), ((), ())),
                            preferred_element_type=F32) * scale
        m = jnp.max(s, axis=-1, keepdims=True)
        p = jnp.exp2(s - m)
        o = jnp.dot(p.astype(BF16), vv, preferred_element_type=F32)
        o = o[:, :MEM_HEAD_DIM] / o[:, MEM_HEAD_DIM:]
        outs.append((o * _silu(g_ref[rows, lo:hi].astype(F32))).astype(BF16))
    return jnp.concatenate(outs, axis=1)


def _out_ln_body(*refs, nparts, scale):
    parts = refs[:nparts]
    q_ref, mg_ref, kv_ref, w_ref, x_ref, g_ref, b_ref, o_ref = refs[nparts:]
    rc = o_ref.shape[0] // OUT_LN_CHUNKS
    for r0 in range(0, o_ref.shape[0], rc):
        rows = pl.ds(r0, rc)
        m_out = _mem_attn(q_ref, mg_ref, kv_ref, rows, scale)
        cat = jnp.concatenate([p[rows, :] for p in parts] + [m_out], axis=1)
        y = jnp.dot(cat, w_ref[...], preferred_element_type=F32)
        z = ALPHA * x_ref[rows, :] + y
        mu = jnp.mean(z, axis=-1, keepdims=True)
        zc = z - mu
        var = jnp.mean(zc * zc, axis=-1, keepdims=True)
        o_ref[rows, :] = zc * lax.rsqrt(var + LN_EPS) * g_ref[...] + b_ref[...]


def _out_ln(parts, h, memkv, q_off, g_off, seq, w, x, g, b, tm=1024, name="out_ln"):
    t = x.shape[0]
    n = len(parts)
    nt = seq // tm
    mlen = memkv.shape[1]
    body = functools.partial(_out_ln_body, nparts=n, scale=MEM_HEAD_DIM ** -0.5 * math.log2(math.e))
    assert sum(p.shape[1] for p in parts) + MEM_DIM == w.shape[0]
    in_specs = [pl.BlockSpec((tm, p.shape[1]), lambda i: (i, 0)) for p in parts]
    in_specs += [pl.BlockSpec((tm, MEM_DIM), lambda i: (i, q_off // MEM_DIM)),
                 pl.BlockSpec((tm, MEM_DIM), lambda i: (i, g_off // MEM_DIM)),
                 pl.BlockSpec((1, mlen, 2 * MEM_DIM), lambda i: (i // nt, 0, 0)),
                 pl.BlockSpec(w.shape, lambda i: (0, 0)),
                 pl.BlockSpec((tm, D_MODEL), lambda i: (i, 0)),
                 pl.BlockSpec((1, D_MODEL), lambda i: (0, 0)),
                 pl.BlockSpec((1, D_MODEL), lambda i: (0, 0))]
    return pl.pallas_call(
        body,
        grid=(t // tm,),
        in_specs=in_specs,
        out_specs=pl.BlockSpec((tm, D_MODEL), lambda i: (i, 0)),
        out_shape=jax.ShapeDtypeStruct((t, D_MODEL), F32),
        compiler_params=_cparams(("parallel",)),
        name=name,
    )(*parts, h, h, memkv, w, x, g, b)


S5_C2 = 8
PW_A8, PW_A64, PW_A512, PW_A1024, PW_A2048 = 0, 1, 2, 3, 4
PW_A64K = 5
PW_A8K = 13
PW_ROWS = 24
S5_PAD = 8


def _cmul(pr, pi, xr, xi):
    return pr * xr - pi * xi, pr * xi + pi * xr


def _s5_scan(s_ref, x_ref, pw_ref, rev):
    half = 512
    n0 = 8 * S5_C2
    n1 = S5_C2

    def pw(row):
        v = pw_ref[pl.ds(row, 1), :]
        return v[:, :half], v[:, half:]

    def blk0(k):
        return (7 - k if rev else k) * n0

    def blk1(k):
        return (7 - k if rev else k) * n1

    def ld(ref, start, n):
        v = ref[pl.ds(start, n), :]
        return v[:, :half], v[:, half:]

    def st(ref, start, n, re, im):
        ref[pl.ds(start, n), 0:half] = re
        ref[pl.ds(start, n), half:2 * half] = im

    a8r, a8i = pw(PW_A8)
    for k in range(1, 8):
        pr_, pi_ = ld(s_ref, blk0(k - 1), n0)
        cr, ci = ld(s_ref, blk0(k), n0)
        mr, mi = _cmul(a8r, a8i, pr_, pi_)
        st(s_ref, blk0(k), n0, cr + mr, ci + mi)
    g0 = blk0(7)
    a64r, a64i = pw(PW_A64)
    for k in range(1, 8):
        pr_, pi_ = ld(s_ref, g0 + blk1(k - 1), n1)
        cr, ci = ld(s_ref, g0 + blk1(k), n1)
        mr, mi = _cmul(a64r, a64i, pr_, pi_)
        st(s_ref, g0 + blk1(k), n1, cr + mr, ci + mi)
    hr, hi = ld(s_ref, g0 + blk1(7), n1)
    c2 = lax.broadcasted_iota(jnp.int32, (n1, half), 0)
    if rev:
        c2 = (S5_C2 - 1) - c2
    for d, row in ((1, PW_A512), (2, PW_A1024), (4, PW_A2048)):
        ar, ai = pw(row)
        sh = (n1 - d) if rev else d
        sr = pltpu.roll(hr, sh, 0)
        si = pltpu.roll(hi, sh, 0)
        mr, mi = _cmul(ar, ai, sr, si)
        keep = c2 >= d
        hr = hr + jnp.where(keep, mr, 0.0)
        hi = hi + jnp.where(keep, mi, 0.0)
    sh1 = (n1 - 1) if rev else 1
    p3r = jnp.where(c2 >= 1, pltpu.roll(hr, sh1, 0), 0.0)
    p3i = jnp.where(c2 >= 1, pltpu.roll(hi, sh1, 0), 0.0)
    for k1 in range(8):
        if k1 == 0:
            er, ei = p3r, p3i
        else:
            ar, ai = pw(PW_A64K + k1 - 1)
            mr, mi = _cmul(ar, ai, p3r, p3i)
            qr, qi = ld(s_ref, g0 + blk1(k1 - 1), n1)
            er, ei = qr + mr, qi + mi
        for k0 in range(8):
            dst = blk0(k0) + blk1(k1)
            if k0 == 0:
                st(x_ref, dst, n1, er, ei)
            else:
                ar, ai = pw(PW_A8K + k0)
                mr, mi = _cmul(ar, ai, er, ei)
                qr, qi = ld(s_ref, blk0(k0 - 1) + blk1(k1), n1)
                st(x_ref, dst, n1, qr + mr, qi + mi)


def _lane_select(d_log2, tt):
    kk = lax.broadcasted_iota(jnp.int32, (LANE, LANE), 0)
    col = tt * LANE + lax.broadcasted_iota(jnp.int32, (LANE, LANE), 1)
    src = lax.shift_left(lax.shift_right_logical(col, d_log2 + 3), d_log2) + (col & ((1 << d_log2) - 1))
    return jnp.where(kk == src, 1.0, 0.0).astype(BF16)


def _expand_blockdiag(comp, m_ref, row0, sels, *, row_b_log2, d_log2):
    rows = comp.shape[0]
    rowg = lax.shift_right_logical(lax.broadcasted_iota(jnp.int32, (rows, LANE), 0), row_b_log2) & 7
    coln = lax.broadcasted_iota(jnp.int32, (rows, LANE), 1)
    for tt in range(8):
        piece = jnp.dot(comp, sels[tt], preferred_element_type=F32)
        colg = lax.shift_right_logical(tt * LANE + coln, d_log2) & 7
        m_ref[pl.ds(row0, rows), tt * LANE:(tt + 1) * LANE] = jnp.where(
            rowg == colg, piece, 0.0).astype(BF16)


_NT = (((1,), (1,)), ((), ()))


def _expand_toeplitz(bx_ref, cax_ref, tw):
    rowg = lax.shift_right_logical(lax.broadcasted_iota(jnp.int32, (LANE, LANE), 0), 4)
    colg = lax.shift_right_logical(lax.broadcasted_iota(jnp.int32, (LANE, LANE), 1), 4)
    keep = rowg == colg

    def kern(n, d):
        return lax.dot_general(bx_ref[d, 0], cax_ref[n, d, 0], _NT,
                               precision=lax.Precision.HIGHEST, preferred_element_type=F32)

    for lag in range(-(S5_L - 1), S5_L):
        if lag == 0:
            k = kern(0, 0) + kern(0, 1)
        else:
            k = kern(lag, 0) if lag > 0 else kern(-lag, 1)
        blk = jnp.where(keep, k, 0.0).astype(BF16)
        for l in range(S5_L):
            lp = l + lag
            if 0 <= lp < S5_L:
                tw[l * LANE:(l + 1) * LANE, lp * LANE:(lp + 1) * LANE] = blk


def _expand_readout(cax_ref, v_ref, d, lags):
    half = 8 * S5_STATE
    row = lax.broadcasted_iota(jnp.int32, (half, LANE), 0)
    lane = lax.broadcasted_iota(jnp.int32, (half, LANE), 1)
    rowg = lax.shift_right_logical(row, 6)
    colg = lax.shift_right_logical(lane, 4)
    for r in range(2):
        pick = jnp.where(((row & (S5_STATE - 1)) + r * S5_STATE) == lane, 1.0, 0.0).astype(BF16)
        for lp, n in enumerate(lags):
            piece = lax.dot_general(pick, cax_ref[n, d, 0].astype(BF16), _NT, preferred_element_type=F32)
            v_ref[r * half:(r + 1) * half, lp * LANE:(lp + 1) * LANE] = jnp.where(
                rowg == colg, piece, 0.0).astype(BF16)


def _s5_body(u_ref, bx_ref, cax_ref, cw_ref, pwf_ref, pwb_ref, d_ref, o_ref,
             tw, wf, wb, vf, vb, stage, xcat, sbuf, xf, xb, *, seq):
    n1 = S5_C2
    blk_rows = seq // S5_C2
    pitch = blk_rows + S5_PAD

    @pl.when(pl.program_id(1) == 0)
    def _():
        _expand_toeplitz(bx_ref, cax_ref, tw)
        sel_w = [_lane_select(6, tt) for tt in range(8)]
        for l in range(S5_L):
            _expand_blockdiag(cw_ref[l, 0, 0], wf, l * LANE, sel_w, row_b_log2=4, d_log2=6)
            _expand_blockdiag(cw_ref[l, 1, 0], wb, l * LANE, sel_w, row_b_log2=4, d_log2=6)
        _expand_readout(cax_ref, vf, 0, [lp + 1 for lp in range(S5_L)])
        _expand_readout(cax_ref, vb, 1, [S5_L - lp for lp in range(S5_L)])

    def pieces():
        for c0 in range(8):
            for c1 in range(8):
                for l in range(S5_L):
                    yield (c0 * 8 * n1 + c1 * n1, l * LANE,
                           pl.ds(c1 * 64 + c0 * 8 + l, n1, stride=pitch))

    for c2 in range(S5_C2):
        stage[pl.ds(c2 * pitch, blk_rows), :] = u_ref[pl.ds(c2 * blk_rows, blk_rows), :]
    for r0, l0, tok in pieces():
        xcat[pl.ds(r0, n1), l0:l0 + LANE] = stage[tok, :]
    xb16 = xcat[...].astype(BF16)
    sbuf[...] = jnp.dot(xb16, wf[...], preferred_element_type=F32)
    _s5_scan(sbuf, xf, pwf_ref, False)
    sbuf[...] = jnp.dot(xb16, wb[...], preferred_element_type=F32)
    _s5_scan(sbuf, xb, pwb_ref, True)
    y = jnp.dot(xb16, tw[...], preferred_element_type=F32)
    y = y + jnp.dot(xf[...].astype(BF16), vf[...], preferred_element_type=F32)
    y = y + jnp.dot(xb[...].astype(BF16), vb[...], preferred_element_type=F32)
    dd = jnp.concatenate([d_ref[...]] * S5_L, axis=-1)
    sbuf[...] = jax.nn.gelu(y + dd * xcat[...])
    for r0, l0, tok in pieces():
        stage[tok, :] = sbuf[pl.ds(r0, n1), l0:l0 + LANE]
    for c2 in range(S5_C2):
        o_ref[pl.ds(c2 * blk_rows, blk_rows), :] = stage[pl.ds(c2 * pitch, blk_rows), :].astype(o_ref.dtype)


def _s5_branch(u, ops, d, bsz, seq):
    t = u.shape[0]
    bx, cax, cw, pw = ops
    ntile = S5_DIM // LANE
    rows = seq // S5_L
    wide = S5_L * LANE
    body = functools.partial(_s5_body, seq=seq)

    def pw_spec(di):
        return pl.BlockSpec((PW_ROWS, wide), lambda j, b: (0, di * ntile + j))

    return pl.pallas_call(
        body,
        grid=(ntile, bsz),
        in_specs=[pl.BlockSpec((seq, LANE), lambda j, b: (b, j)),
                  pl.BlockSpec((2, 1, LANE, LANE), lambda j, b: (0, j, 0, 0)),
                  pl.BlockSpec((S5_L + 1, 2, 1, LANE, LANE), lambda j, b: (0, 0, j, 0, 0)),
                  pl.BlockSpec((S5_L, 2, 1, LANE, LANE), lambda j, b: (0, 0, j, 0, 0)),
                  pw_spec(0), pw_spec(1),
                  pl.BlockSpec((1, LANE), lambda j, b: (0, j))],
        out_specs=pl.BlockSpec((seq, LANE), lambda j, b: (b, j)),
        out_shape=jax.ShapeDtypeStruct((t, S5_DIM), BF16),
        scratch_shapes=([pltpu.VMEM((wide, wide), BF16)] * 5
                        + [pltpu.VMEM((seq + S5_C2 * S5_PAD, LANE), F32)]
                        + [pltpu.VMEM((rows, wide), F32)] * 4),
        compiler_params=_cparams(("parallel", "arbitrary")),
        name="s5_branch",
    )(u, bx, cax, cw, pw, pw, d)


def _s5_prep(fwd, bwd):
    G, P, C, L = S5_GROUPS, S5_STATE, S5_GROUP, S5_L
    nt = G // 8
    a_re, a_im, log_dt, b_re, b_im, c_re, c_im = (jnp.stack([f, b]) for f, b in zip(fwd, bwd))
    lam_re = jnp.minimum(a_re, -1e-4)
    lam_im = a_im
    dt = jnp.exp(log_dt)[..., None]
    mag = jnp.exp(lam_re * dt)
    lb_re = mag * jnp.cos(lam_im * dt)
    lb_im = mag * jnp.sin(lam_im * dt)
    den = lam_re * lam_re + lam_im * lam_im
    nr = lb_re - 1.0
    f_re = (nr * lam_re + lb_im * lam_im) / den
    f_im = (lb_im * lam_re - nr * lam_im) / den
    bt_re = jnp.swapaxes(b_re, -1, -2)
    bt_im = jnp.swapaxes(b_im, -1, -2)
    bbt_re = f_re[:, :, None, :] * bt_re - f_im[:, :, None, :] * bt_im
    bbt_im = f_re[:, :, None, :] * bt_im + f_im[:, :, None, :] * bt_re

    def power(n):
        n = jnp.asarray(n, F32).reshape(-1, 1, 1, 1)
        m = jnp.exp(n * (lam_re * dt))
        ang = n * (lam_im * dt)
        return m * jnp.cos(ang), m * jnp.sin(ang)

    pr, pi = power(jnp.arange(L + 1))

    pr2 = jnp.concatenate([pr, -pi], axis=-1)[:, :, :, None, :]
    pi2 = jnp.concatenate([-pi, -pr], axis=-1)[:, :, :, None, :]
    c_ri = jnp.concatenate([c_re, c_re], axis=-1)[None]
    c_ii = jnp.concatenate([c_im, c_im], axis=-1)[None]
    cax = (c_ri * pr2 + c_ii * pi2).reshape(L + 1, 2, nt, 8 * C, 2 * P)
    bx = jnp.concatenate([bbt_re, bbt_im], axis=-1)

    prw = jnp.stack([pr[L - 1::-1, 0], pr[:L, 1]], axis=1)
    piw = jnp.stack([pi[L - 1::-1, 0], pi[:L, 1]], axis=1)
    aa = jnp.concatenate([prw, prw], axis=-1)[:, :, :, None, :]
    ab = jnp.concatenate([-piw, piw], axis=-1)[:, :, :, None, :]
    x2 = jnp.concatenate([bbt_im, bbt_re], axis=-1)[None]
    cw = (aa * bx[None] + ab * x2).astype(BF16).reshape(L, 2, nt, 8 * C, 2 * P)
    bx = bx.reshape(2, nt, 8 * C, 2 * P)

    ns = ([8.0, 64.0, 512.0, 1024.0, 2048.0] + [64.0 * (k + 1) for k in range(8)]
          + [8.0 * k for k in range(8)] + [0.0] * (PW_ROWS - 21))
    qr, qi = power(ns)
    pw = jnp.concatenate([qr.reshape(PW_ROWS, 2, nt, 8 * P), qi.reshape(PW_ROWS, 2, nt, 8 * P)],
                         axis=-1).reshape(PW_ROWS, 2 * nt * 2 * 8 * P)
    return bx, cax, cw, pw


GLU_COLS = 256


def _glu_mm_body(a_ref, w_ref, g_ref, o_ref):
    a = a_ref[...]
    n = o_ref.shape[1]
    for c0 in range(0, n, GLU_COLS):
        z1 = jnp.dot(a, w_ref[:, c0:c0 + GLU_COLS], preferred_element_type=F32)
        z2 = jnp.dot(a, w_ref[:, n + c0:n + c0 + GLU_COLS], preferred_element_type=F32)
        gate = _silu(g_ref[:, c0:c0 + GLU_COLS].astype(F32))
        o_ref[:, c0:c0 + GLU_COLS] = (z1 * jax.nn.sigmoid(z2) * gate).astype(o_ref.dtype)


def _glu_matmul(a, w, h2, tm=1024):
    t, k = a.shape
    n = w.shape[1] // 2
    return pl.pallas_call(
        _glu_mm_body,
        grid=(t // tm,),
        in_specs=[pl.BlockSpec((tm, k), lambda i: (i, 0)),
                  pl.BlockSpec((k, 2 * n), lambda i: (0, 0)),
                  pl.BlockSpec((tm, n), lambda i: (i, O_GATE // n))],
        out_specs=pl.BlockSpec((tm, n), lambda i: (i, 0)),
        out_shape=jax.ShapeDtypeStruct((t, n), BF16),
        compiler_params=_cparams(("parallel",)),
        name="s5_glu",
    )(a, w, h2)


def _even_in_weight(w_in):
    w = w_in.astype(BF16)
    k = w.shape[0]
    k_rope = w[:, E_HEAD:E_HEAD + QK_ROPE]
    z = jnp.zeros((k, LANE - QK_ROPE), BF16)
    return w, jnp.concatenate([w[:, E_HEAD + QK_ROPE:], k_rope, z], axis=1)


def _uq_weight(w_uq):
    k = w_uq.shape[0]
    w = w_uq.reshape(k, MLA_HEADS, QK_NOPE + QK_ROPE)
    nope = w[:, :, :QK_NOPE].reshape(k, MLA_HEADS * QK_NOPE)
    rope = w[:, :, QK_NOPE:].reshape(k, MLA_HEADS * QK_ROPE)
    return jnp.concatenate([nope, rope], axis=1).astype(BF16)


def _ukv_weight(w_ukv):
    k = w_ukv.shape[0]
    w = w_ukv.reshape(k, MLA_HEADS, QK_NOPE + V_DIM)
    kn = w[:, :, :QK_NOPE].reshape(k, MLA_HEADS * QK_NOPE)
    vv = w[:, :, QK_NOPE:].reshape(k, MLA_HEADS * V_DIM)
    return jnp.concatenate([kn, vv], axis=1).astype(BF16)


def _rope_tables(positions):
    inv_freq = ROPE_THETA ** (-jnp.arange(0, QK_ROPE, 2, dtype=F32) / QK_ROPE)
    half = QK_ROPE // 2
    inv4 = jnp.tile(inv_freq, LANE // half)
    sign = jnp.tile(jnp.concatenate([-jnp.ones((half,), F32), jnp.ones((half,), F32)]),
                    LANE // QK_ROPE)
    ang = positions.astype(F32)[..., None] * inv4
    return jnp.cos(ang), jnp.sin(ang) * sign


def _row(v):
    return v.reshape(1, -1)


def _even_layer(x2, mem2, cos128, sin128, bsz, seq, w_in, conv_w, conv_b, conv_ln_g, conv_ln_b,
                q_norm, w_uq, kv_norm, w_ukv, w_mem_kv, w_out, ln_g, ln_b):
    taps = jnp.broadcast_to(conv_w.reshape(CONV_WIDTH, 1, CONV_DIM), (CONV_WIDTH, SUBLANE, CONV_DIM))
    h, a_out = _in_proj_conv(x2, *_even_in_weight(w_in), taps, _row(conv_b), _row(conv_ln_g),
                             _row(conv_ln_b), seq)
    q, k, v = _mla_proj(h, cos128, sin128, _row(q_norm), _row(kv_norm), _uq_weight(w_uq),
                        _ukv_weight(w_ukv), bsz, seq)
    b_out = _mla_attention(q, k, v, h)
    (memkv,) = _proj(mem2, w_mem_kv.astype(BF16), [(2 * MEM_DIM, BF16)], 512, "even_mem_kv")
    memkv = memkv.reshape(bsz, -1, 2 * MEM_DIM)
    return _out_ln([a_out, b_out], h, memkv, E_MEM_Q, E_MEM_GATE, seq, w_out.astype(BF16), x2,
                   _row(ln_g), _row(ln_b), name="even_out_ln")


def _odd_layer(x2, mem2, bsz, seq, w_in, s5_fwd, s5_bwd, s5_d, w_glu, w_mem_kv, w_out, ln_g, ln_b):
    n_in = w_in.shape[1]
    u, h2 = _proj(x2, w_in.astype(BF16), [(O_SPLIT, F32), (n_in - O_SPLIT, BF16)], 1024,
                  "odd_in_proj")
    g = _s5_branch(u, _s5_prep(s5_fwd, s5_bwd), _row(s5_d), bsz, seq)
    c_out = _glu_matmul(g, w_glu.astype(BF16), h2)
    (memkv,) = _proj(mem2, w_mem_kv.astype(BF16), [(2 * MEM_DIM, BF16)], 512, "odd_mem_kv")
    memkv = memkv.reshape(bsz, -1, 2 * MEM_DIM)
    return _out_ln([c_out], h2, memkv, O_MEM_Q, O_MEM_GATE, seq, w_out.astype(BF16), x2,
                   _row(ln_g), _row(ln_b), name="odd_out_ln")


def kernel(x, mem, positions, e_w_in, e_conv_w, e_conv_b, e_conv_ln_g, e_conv_ln_b, e_q_norm, e_w_uq, e_kv_norm, e_w_ukv, e_mem_kv, e_w_out, e_ln_g, e_ln_b, o_w_in, o_a_re_f, o_a_im_f, o_log_dt_f, o_b_re_f, o_b_im_f, o_c_re_f, o_c_im_f, o_a_re_b, o_a_im_b, o_log_dt_b, o_b_re_b, o_b_im_b, o_c_re_b, o_c_im_b, o_d, o_w_glu, o_mem_kv, o_w_out, o_ln_g, o_ln_b):
    bsz, seq, d = x.shape
    cos128, sin128 = _rope_tables(positions)
    x2 = x.reshape(bsz * seq, d)
    mem2 = mem.reshape(-1, d)
    h = _even_layer(x2, mem2, cos128, sin128, bsz, seq, e_w_in[0], e_conv_w[0], e_conv_b[0],
                    e_conv_ln_g[0], e_conv_ln_b[0], e_q_norm[0], e_w_uq[0], e_kv_norm[0],
                    e_w_ukv[0], e_mem_kv[0], e_w_out[0], e_ln_g[0], e_ln_b[0])
    s5_fwd = (o_a_re_f[0], o_a_im_f[0], o_log_dt_f[0], o_b_re_f[0], o_b_im_f[0], o_c_re_f[0], o_c_im_f[0])
    s5_bwd = (o_a_re_b[0], o_a_im_b[0], o_log_dt_b[0], o_b_re_b[0], o_b_im_b[0], o_c_re_b[0], o_c_im_b[0])
    h = _odd_layer(h, mem2, bsz, seq, o_w_in[0], s5_fwd, s5_bwd, o_d[0], o_w_glu[0], o_mem_kv[0],
                   o_w_out[0], o_ln_g[0], o_ln_b[0])
    return h.reshape(bsz, seq, d)
```

```python
import functools
import math

import jax
import jax.numpy as jnp
from jax import lax
from jax.experimental import pallas as pl
from jax.experimental.pallas import tpu as pltpu

F32 = jnp.float32
BF16 = jnp.bfloat16

D_MODEL = 1024
CONV_DIM = 1024
CONV_WIDTH = 31
CONV_PAD = 15
MLA_HEADS = 8
QK_NOPE = 128
QK_ROPE = 64
V_DIM = 128
Q_LORA = 768
KV_LORA = 256
ROPE_THETA = 10000.0
MEM_HEADS = 4
MEM_HEAD_DIM = 128
MEM_DIM = 512
S5_DIM = 1024
S5_GROUP = 16
S5_GROUPS = 64
S5_STATE = 64
LN_EPS = 1e-5
RMS_EPS = 1e-6
DEPTH = 2
ALPHA = (2 * DEPTH) ** 0.25

LANE = 128
SUBLANE = 8
VMEM_LIMIT = 56 * 1024 * 1024

E_CONV = 3 * CONV_DIM
E_CQ = 0
E_CKV = E_CQ + Q_LORA
E_MLA_GATE = E_CKV + KV_LORA
E_MEM_Q = E_MLA_GATE + MLA_HEADS * V_DIM
E_MEM_GATE = E_MEM_Q + MEM_DIM
E_KROPE = E_MEM_GATE + MEM_DIM
E_NREST = E_KROPE + LANE
E_HEAD = E_CONV + Q_LORA + KV_LORA
O_SPLIT = S5_DIM
O_GATE = 0
O_MEM_Q = O_GATE + S5_DIM
O_MEM_GATE = O_MEM_Q + MEM_DIM

S5_L = 8


def _cparams(sem):
    return pltpu.CompilerParams(dimension_semantics=sem, vmem_limit_bytes=VMEM_LIMIT)


def _silu(x):
    return x * jax.nn.sigmoid(x)


def _proj_body(a_ref, w_ref, *o_refs, bounds):
    y = jnp.dot(a_ref[...].astype(BF16), w_ref[...], preferred_element_type=F32)
    for o_ref, (lo, hi) in zip(o_refs, bounds):
        o_ref[...] = y[:, lo:hi].astype(o_ref.dtype)


def _proj(a, w, outs, tm, name):
    m, k = a.shape
    n = w.shape[1]
    tm = min(tm, m)
    assert m % tm == 0 and sum(width for width, _ in outs) == n
    bounds, lo = [], 0
    for width, _ in outs:
        bounds.append((lo, lo + width))
        lo += width
    res = pl.pallas_call(
        functools.partial(_proj_body, bounds=tuple(bounds)),
        grid=(m // tm,),
        in_specs=[pl.BlockSpec((tm, k), lambda i: (i, 0)),
                  pl.BlockSpec((k, n), lambda i: (0, 0))],
        out_specs=[pl.BlockSpec((tm, width), lambda i: (i, 0)) for width, _ in outs],
        out_shape=[jax.ShapeDtypeStruct((m, width), dt) for width, dt in outs],
        compiler_params=_cparams(("parallel",)),
        name=name,
    )(a, w)
    return res


CONV_HALO = 16
CONV_RC = 32


def _in_conv_body(x_ref, wa_ref, wb_ref, cw, cb, lg, lb, h_ref, o_ref, xb, slab, cur, gcur, gate, tail,
                  *, ts, tiles_per_seq):
    i = pl.program_id(0)

    @pl.when(i == 0)
    def _():
        cur[...] = jnp.zeros_like(cur)
        gcur[...] = jnp.zeros_like(gcur)
        tail[...] = jnp.zeros_like(tail)

    xb[...] = x_ref[...].astype(BF16)
    na = wa_ref.shape[1]
    n = na + wb_ref.shape[1]

    def proj(lo, hi):
        assert hi <= na or lo >= na
        w = wa_ref[:, lo:hi] if hi <= na else wb_ref[:, lo - na:hi - na]
        return jnp.dot(xb[...], w, preferred_element_type=F32)

    glu_new = proj(0, CONV_DIM) * jax.nn.sigmoid(proj(CONV_DIM, 2 * CONV_DIM))

    j = i - 1
    first = (j % tiles_per_seq) == 0
    last = (j % tiles_per_seq) == tiles_per_seq - 1
    slab[0, pl.ds(CONV_HALO, ts), :] = cur[...]
    slab[0, pl.ds(0, CONV_HALO), :] = jnp.where(first, 0.0, tail[...])
    slab[0, pl.ds(CONV_HALO + ts, CONV_HALO), :] = jnp.where(last, 0.0, glu_new[0:CONV_HALO, :])
    gate[...] = gcur[...]
    tail[...] = cur[pl.ds(ts - CONV_HALO, CONV_HALO), :]
    cur[...] = glu_new

    rows = ts + 2 * CONV_HALO
    for lt in range(CONV_DIM // LANE):
        cols = slice(lt * LANE, (lt + 1) * LANE)
        base = slab[0, :, cols]
        for r in range(1, SUBLANE):
            slab[r, :, cols] = pltpu.roll(base, rows - r, 0)

    nchunks = ts // CONV_RC
    step_cols = (n - 2 * CONV_DIM) // nchunks // LANE * LANE
    off = CONV_HALO - CONV_PAD
    for rc in range(nchunks):
        lo = 2 * CONV_DIM + rc * step_cols
        hi = n if rc == nchunks - 1 else lo + step_cols
        yp = proj(lo, hi)
        if hi <= E_CONV:
            gcur[:, lo - 2 * CONV_DIM:hi - 2 * CONV_DIM] = yp.astype(gcur.dtype)
        else:
            assert lo >= E_CONV
            h_ref[:, lo - E_CONV:hi - E_CONV] = yp.astype(h_ref.dtype)

        base = rc * CONV_RC
        acc = jnp.zeros((CONV_RC, CONV_DIM), F32)
        for k in range(CONV_WIDTH):
            r = (off + k) % SUBLANE
            wk = jnp.concatenate([cw[k]] * (CONV_RC // SUBLANE), axis=0)
            acc = acc + slab[r, pl.ds(base + off + k - r, CONV_RC), :] * wk
        dw = acc + cb[...]
        mu = jnp.mean(dw, axis=-1, keepdims=True)
        xc = dw - mu
        var = jnp.mean(xc * xc, axis=-1, keepdims=True)
        yn = xc * lax.rsqrt(var + LN_EPS) * lg[...] + lb[...]
        res = _silu(yn) * _silu(gate[pl.ds(base, CONV_RC), :].astype(F32))
        o_ref[pl.ds(base, CONV_RC), :] = res.astype(o_ref.dtype)


def _in_proj_conv(x, w_full, w_tail, taps, conv_b, ln_g, ln_b, seq, ts=256):
    t, k = x.shape
    n = E_HEAD + w_tail.shape[1]
    nt = t // ts
    body = functools.partial(_in_conv_body, ts=ts, tiles_per_seq=seq // ts)
    vec = pl.BlockSpec((1, CONV_DIM), lambda i: (0, 0))
    return pl.pallas_call(
        body,
        grid=(nt + 1,),
        in_specs=[
            pl.BlockSpec((ts, k), lambda i: (jnp.minimum(i, nt - 1), 0)),
            pl.BlockSpec((k, E_HEAD), lambda i: (0, 0)),
            pl.BlockSpec(w_tail.shape, lambda i: (0, 0)),
            pl.BlockSpec((CONV_WIDTH, SUBLANE, CONV_DIM), lambda i: (0, 0, 0)),
            vec, vec, vec,
        ],
        out_specs=[pl.BlockSpec((ts, n - E_CONV), lambda i: (jnp.minimum(i, nt - 1), 0)),
                   pl.BlockSpec((ts, CONV_DIM), lambda i: (jnp.maximum(i - 1, 0), 0))],
        out_shape=[jax.ShapeDtypeStruct((t, n - E_CONV), BF16),
                   jax.ShapeDtypeStruct((t, CONV_DIM), BF16)],
        scratch_shapes=[pltpu.VMEM((ts, k), BF16),
                        pltpu.VMEM((SUBLANE, ts + 2 * CONV_HALO, CONV_DIM), F32),
                        pltpu.VMEM((ts, CONV_DIM), F32),
                        pltpu.VMEM((ts, CONV_DIM), BF16),
                        pltpu.VMEM((ts, CONV_DIM), BF16),
                        pltpu.VMEM((CONV_HALO, CONV_DIM), F32)],
        compiler_params=_cparams(("arbitrary",)),
        name="in_proj_conv",
    )(x, w_full, w_tail, taps, conv_b, ln_g, ln_b)


def _rms(x, g):
    ms = jnp.mean(x * x, axis=-1, keepdims=True)
    return x * lax.rsqrt(ms + RMS_EPS) * g


MLA_PROJ_CHUNKS = 2


def _rope(x, c, s):
    half = QK_ROPE // 2
    lane = lax.broadcasted_iota(jnp.int32, x.shape, 1)
    swapped = jnp.where((lane & (QK_ROPE - 1)) < half,
                        pltpu.roll(x, LANE - half, 1),
                        pltpu.roll(x, half, 1))
    return x * c + swapped * s


def _mla_proj_body(cq, ckv, kr, cos, sin, qn_g, kvn_g, wq, wkv, q_ref, k_ref, v_ref, *, scale):
    nh = MLA_HEADS * QK_NOPE
    nr = MLA_HEADS * QK_ROPE
    rc = cq.shape[0] // MLA_PROJ_CHUNKS
    for r0 in range(0, cq.shape[0], rc):
        rows = pl.ds(r0, rc)
        c = cos[0, rows, :]
        s = sin[0, rows, :]
        nq = _rms(cq[rows, :].astype(F32), qn_g[...]).astype(BF16)
        qf = jnp.dot(nq, wq[...], preferred_element_type=F32)
        qr = [_rope(qf[:, nh + p * LANE:nh + (p + 1) * LANE], c, s) for p in range(nr // LANE)]
        nkv = _rms(ckv[rows, :].astype(F32), kvn_g[...]).astype(BF16)
        kvf = jnp.dot(nkv, wkv[...], preferred_element_type=F32)
        kr_even = _rope(kr[rows, :].astype(F32), c, s)
        kr_odd = pltpu.roll(kr_even, QK_ROPE, 1)
        for h in range(MLA_HEADS):
            q_ref[0, h, rows, 0:QK_NOPE] = (qf[:, h * QK_NOPE:(h + 1) * QK_NOPE] * scale).astype(BF16)
            q_ref[0, h, rows, QK_NOPE:2 * QK_NOPE] = (qr[h // 2] * scale).astype(BF16)
            k_ref[0, h, rows, 0:QK_NOPE] = kvf[:, h * QK_NOPE:(h + 1) * QK_NOPE].astype(BF16)
            k_ref[0, h, rows, QK_NOPE:2 * QK_NOPE] = (kr_even if h % 2 == 0 else kr_odd).astype(BF16)
            v_ref[0, h, rows, :] = kvf[:, nh + h * V_DIM:nh + (h + 1) * V_DIM].astype(BF16)


def _mla_proj(h, cos128, sin128, q_norm, kv_norm, wq, wkv, bsz, seq, tm=1024):
    nt = seq // tm
    scale = (QK_NOPE + QK_ROPE) ** -0.5 * math.log2(math.e)
    body = functools.partial(_mla_proj_body, scale=scale)
    hd = 2 * QK_NOPE
    return pl.pallas_call(
        body,
        grid=(bsz, nt),
        in_specs=[
            pl.BlockSpec((tm, Q_LORA), lambda b, i: (b * nt + i, E_CQ // Q_LORA)),
            pl.BlockSpec((tm, KV_LORA), lambda b, i: (b * nt + i, E_CKV // KV_LORA)),
            pl.BlockSpec((tm, LANE), lambda b, i: (b * nt + i, E_KROPE // LANE)),
            pl.BlockSpec((1, tm, LANE), lambda b, i: (b, i, 0)),
            pl.BlockSpec((1, tm, LANE), lambda b, i: (b, i, 0)),
            pl.BlockSpec((1, Q_LORA), lambda b, i: (0, 0)),
            pl.BlockSpec((1, KV_LORA), lambda b, i: (0, 0)),
            pl.BlockSpec(wq.shape, lambda b, i: (0, 0)),
            pl.BlockSpec(wkv.shape, lambda b, i: (0, 0)),
        ],
        out_specs=[
            pl.BlockSpec((1, MLA_HEADS, tm, hd), lambda b, i: (b, 0, i, 0)),
            pl.BlockSpec((1, MLA_HEADS, tm, hd), lambda b, i: (b, 0, i, 0)),
            pl.BlockSpec((1, MLA_HEADS, tm, V_DIM), lambda b, i: (b, 0, i, 0)),
        ],
        out_shape=[
            jax.ShapeDtypeStruct((bsz, MLA_HEADS, seq, hd), BF16),
            jax.ShapeDtypeStruct((bsz, MLA_HEADS, seq, hd), BF16),
            jax.ShapeDtypeStruct((bsz, MLA_HEADS, seq, V_DIM), BF16),
        ],
        compiler_params=_cparams(("parallel", "parallel")),
        name="mla_proj",
    )(h, h, h, cos128, sin128, q_norm, kv_norm, wq, wkv)


def _flash_body(q_ref, k_ref, v_ref, g_ref, o_ref, *, tk, nk, unroll):
    q = q_ref[0, 0]
    tq = q.shape[0]

    ones = jnp.ones((tk, V_DIM), BF16)

    def step(c, carry):
        m, acc = carry
        start = pl.multiple_of(c * tk, tk)
        ks = k_ref[0, 0, pl.ds(start, tk), :]
        vs = jnp.concatenate([v_ref[0, 0, pl.ds(start, tk), :], ones], axis=1)
        s = lax.dot_general(q, ks, (((1,), (1,)), ((), ())), preferred_element_type=F32)
        m_new = jnp.maximum(m, jnp.max(s, axis=-1, keepdims=True))
        alpha = jnp.exp2(m - m_new)
        p = jnp.exp2(s - m_new)
        acc = alpha * acc + jnp.dot(p.astype(BF16), vs, preferred_element_type=F32)
        return m_new, acc

    m0 = jnp.full((tq, 1), -jnp.inf, F32)
    a0 = jnp.zeros((tq, 2 * V_DIM), F32)
    m, acc = lax.fori_loop(0, nk, step, (m0, a0), unroll=unroll)
    o_ref[...] = (acc[:, :V_DIM] / acc[:, V_DIM:] * _silu(g_ref[...].astype(F32))).astype(o_ref.dtype)


def _mla_attention(q, k, v, h, tq=2048, tk=256, unroll=True):
    bsz, heads, seq, hd = q.shape
    nq = seq // tq
    body = functools.partial(_flash_body, tk=tk, nk=seq // tk, unroll=unroll)
    gcol = E_MLA_GATE // V_DIM
    return pl.pallas_call(
        body,
        grid=(bsz, heads, nq),
        in_specs=[
            pl.BlockSpec((1, 1, tq, hd), lambda b, hh, i: (b, hh, i, 0)),
            pl.BlockSpec((1, 1, seq, hd), lambda b, hh, i: (b, hh, 0, 0)),
            pl.BlockSpec((1, 1, seq, V_DIM), lambda b, hh, i: (b, hh, 0, 0)),
            pl.BlockSpec((tq, V_DIM), lambda b, hh, i: (b * nq + i, gcol + hh)),
        ],
        out_specs=pl.BlockSpec((tq, V_DIM), lambda b, hh, i: (b * nq + i, hh)),
        out_shape=jax.ShapeDtypeStruct((bsz * seq, heads * V_DIM), BF16),
        compiler_params=_cparams(("parallel", "parallel", "arbitrary")),
        name="mla_attention",
    )(q, k, v, h)


OUT_LN_CHUNKS = 4


def _mem_attn(q_ref, g_ref, kv_ref, rows, scale):
    ones = jnp.ones((kv_ref.shape[1], MEM_HEAD_DIM), BF16)
    outs = []
    for hh in range(MEM_HEADS):
        lo, hi = hh * MEM_HEAD_DIM, (hh + 1) * MEM_HEAD_DIM
        kk = kv_ref[0, :, lo:hi]
        vv = jnp.concatenate([kv_ref[0, :, MEM_DIM + lo:MEM_DIM + hi], ones], axis=1)
        s = lax.dot_general(q_ref[rows, lo:hi], kk, (((1,), (1,)), ((), ())),
                            preferred_element_type=F32) * scale
        m = jnp.max(s, axis=-1, keepdims=True)
        p = jnp.exp2(s - m)
        o = jnp.dot(p.astype(BF16), vv, preferred_element_type=F32)
        o = o[:, :MEM_HEAD_DIM] / o[:, MEM_HEAD_DIM:]
        outs.append((o * _silu(g_ref[rows, lo:hi].astype(F32))).astype(BF16))
    return jnp.concatenate(outs, axis=1)


def _out_ln_body(*refs, nparts, scale):
    parts = refs[:nparts]
    q_ref, mg_ref, kv_ref, w_ref, x_ref, g_ref, b_ref, o_ref = refs[nparts:]
    rc = o_ref.shape[0] // OUT_LN_CHUNKS
    for r0 in range(0, o_ref.shape[0], rc):
        rows = pl.ds(r0, rc)
        m_out = _mem_attn(q_ref, mg_ref, kv_ref, rows, scale)
        cat = jnp.concatenate([p[rows, :] for p in parts] + [m_out], axis=1)
        y = jnp.dot(cat, w_ref[...], preferred_element_type=F32)
        z = ALPHA * x_ref[rows, :] + y
        mu = jnp.mean(z, axis=-1, keepdims=True)
        zc = z - mu
        var = jnp.mean(zc * zc, axis=-1, keepdims=True)
        o_ref[rows, :] = zc * lax.rsqrt(var + LN_EPS) * g_ref[...] + b_ref[...]


def _out_ln(parts, h, memkv, q_off, g_off, seq, w, x, g, b, tm=1024, name="out_ln"):
    t = x.shape[0]
    n = len(parts)
    nt = seq // tm
    mlen = memkv.shape[1]
    body = functools.partial(_out_ln_body, nparts=n, scale=MEM_HEAD_DIM ** -0.5 * math.log2(math.e))
    assert sum(p.shape[1] for p in parts) + MEM_DIM == w.shape[0]
    in_specs = [pl.BlockSpec((tm, p.shape[1]), lambda i: (i, 0)) for p in parts]
    in_specs += [pl.BlockSpec((tm, MEM_DIM), lambda i: (i, q_off // MEM_DIM)),
                 pl.BlockSpec((tm, MEM_DIM), lambda i: (i, g_off // MEM_DIM)),
                 pl.BlockSpec((1, mlen, 2 * MEM_DIM), lambda i: (i // nt, 0, 0)),
                 pl.BlockSpec(w.shape, lambda i: (0, 0)),
                 pl.BlockSpec((tm, D_MODEL), lambda i: (i, 0)),
                 pl.BlockSpec((1, D_MODEL), lambda i: (0, 0)),
                 pl.BlockSpec((1, D_MODEL), lambda i: (0, 0))]
    return pl.pallas_call(
        body,
        grid=(t // tm,),
        in_specs=in_specs,
        out_specs=pl.BlockSpec((tm, D_MODEL), lambda i: (i, 0)),
        out_shape=jax.ShapeDtypeStruct((t, D_MODEL), F32),
        compiler_params=_cparams(("parallel",)),
        name=name,
    )(*parts, h, h, memkv, w, x, g, b)


S5_C2 = 8
PW_A8, PW_A64, PW_A512, PW_A1024, PW_A2048 = 0, 1, 2, 3, 4
PW_A64K = 5
PW_A8K = 13
PW_ROWS = 24
S5_PAD = 8


def _cmul(pr, pi, xr, xi):
    return pr * xr - pi * xi, pr * xi + pi * xr


def _s5_scan(s_ref, x_ref, pw_ref, rev):
    half = 512
    n0 = 8 * S5_C2
    n1 = S5_C2

    def pw(row):
        v = pw_ref[pl.ds(row, 1), :]
        return v[:, :half], v[:, half:]

    def blk0(k):
        return (7 - k if rev else k) * n0

    def blk1(k):
        return (7 - k if rev else k) * n1

    def ld(ref, start, n):
        v = ref[pl.ds(start, n), :]
        return v[:, :half], v[:, half:]

    def st(ref, start, n, re, im):
        ref[pl.ds(start, n), 0:half] = re
        ref[pl.ds(start, n), half:2 * half] = im

    a8r, a8i = pw(PW_A8)
    for k in range(1, 8):
        pr_, pi_ = ld(s_ref, blk0(k - 1), n0)
        cr, ci = ld(s_ref, blk0(k), n0)
        mr, mi = _cmul(a8r, a8i, pr_, pi_)
        st(s_ref, blk0(k), n0, cr + mr, ci + mi)
    g0 = blk0(7)
    a64r, a64i = pw(PW_A64)
    for k in range(1, 8):
        pr_, pi_ = ld(s_ref, g0 + blk1(k - 1), n1)
        cr, ci = ld(s_ref, g0 + blk1(k), n1)
        mr, mi = _cmul(a64r, a64i, pr_, pi_)
        st(s_ref, g0 + blk1(k), n1, cr + mr, ci + mi)
    hr, hi = ld(s_ref, g0 + blk1(7), n1)
    c2 = lax.broadcasted_iota(jnp.int32, (n1, half), 0)
    if rev:
        c2 = (S5_C2 - 1) - c2
    for d, row in ((1, PW_A512), (2, PW_A1024), (4, PW_A2048)):
        ar, ai = pw(row)
        sh = (n1 - d) if rev else d
        sr = pltpu.roll(hr, sh, 0)
        si = pltpu.roll(hi, sh, 0)
        mr, mi = _cmul(ar, ai, sr, si)
        keep = c2 >= d
        hr = hr + jnp.where(keep, mr, 0.0)
        hi = hi + jnp.where(keep, mi, 0.0)
    sh1 = (n1 - 1) if rev else 1
    p3r = jnp.where(c2 >= 1, pltpu.roll(hr, sh1, 0), 0.0)
    p3i = jnp.where(c2 >= 1, pltpu.roll(hi, sh1, 0), 0.0)
    for k1 in range(8):
        if k1 == 0:
            er, ei = p3r, p3i
        else:
            ar, ai = pw(PW_A64K + k1 - 1)
            mr, mi = _cmul(ar, ai, p3r, p3i)
            qr, qi = ld(s_ref, g0 + blk1(k1 - 1), n1)
            er, ei = qr + mr, qi + mi
        for k0 in range(8):
            dst = blk0(k0) + blk1(k1)
            if k0 == 0:
                st(x_ref, dst, n1, er, ei)
            else:
                ar, ai = pw(PW_A8K + k0)
                mr, mi = _cmul(ar, ai, er, ei)
                qr, qi = ld(s_ref, blk0(k0 - 1) + blk1(k1), n1)
                st(x_ref, dst, n1, qr + mr, qi + mi)


def _lane_select(d_log2, tt):
    kk = lax.broadcasted_iota(jnp.int32, (LANE, LANE), 0)
    col = tt * LANE + lax.broadcasted_iota(jnp.int32, (LANE, LANE), 1)
    src = lax.shift_left(lax.shift_right_logical(col, d_log2 + 3), d_log2) + (col & ((1 << d_log2) - 1))
    return jnp.where(kk == src, 1.0, 0.0).astype(BF16)


def _expand_blockdiag(comp, m_ref, row0, sels, *, row_b_log2, d_log2):
    rows = comp.shape[0]
    rowg = lax.shift_right_logical(lax.broadcasted_iota(jnp.int32, (rows, LANE), 0), row_b_log2) & 7
    coln = lax.broadcasted_iota(jnp.int32, (rows, LANE), 1)
    for tt in range(8):
        piece = jnp.dot(comp, sels[tt], preferred_element_type=F32)
        colg = lax.shift_right_logical(tt * LANE + coln, d_log2) & 7
        m_ref[pl.ds(row0, rows), tt * LANE:(tt + 1) * LANE] = jnp.where(
            rowg == colg, piece, 0.0).astype(BF16)


_NT = (((1,), (1,)), ((), ()))


def _expand_toeplitz(bx_ref, cax_ref, tw):
    rowg = lax.shift_right_logical(lax.broadcasted_iota(jnp.int32, (LANE, LANE), 0), 4)
    colg = lax.shift_right_logical(lax.broadcasted_iota(jnp.int32, (LANE, LANE), 1), 4)
    keep = rowg == colg

    def kern(n, d):
        return lax.dot_general(bx_ref[d, 0], cax_ref[n, d, 0], _NT,
                               precision=lax.Precision.HIGHEST, preferred_element_type=F32)

    for lag in range(-(S5_L - 1), S5_L):
        if lag == 0:
            k = kern(0, 0) + kern(0, 1)
        else:
            k = kern(lag, 0) if lag > 0 else kern(-lag, 1)
        blk = jnp.where(keep, k, 0.0).astype(BF16)
        for l in range(S5_L):
            lp = l + lag
            if 0 <= lp < S5_L:
                tw[l * LANE:(l + 1) * LANE, lp * LANE:(lp + 1) * LANE] = blk


def _expand_readout(cax_ref, v_ref, d, lags):
    half = 8 * S5_STATE
    row = lax.broadcasted_iota(jnp.int32, (half, LANE), 0)
    lane = lax.broadcasted_iota(jnp.int32, (half, LANE), 1)
    rowg = lax.shift_right_logical(row, 6)
    colg = lax.shift_right_logical(lane, 4)
    stacked = jnp.concatenate([cax_ref[n, d, 0].astype(BF16) for n in lags], axis=0)
    for r in range(2):
        pick = jnp.where(((row & (S5_STATE - 1)) + r * S5_STATE) == lane, 1.0, 0.0).astype(BF16)
        piece = lax.dot_general(pick, stacked, _NT, preferred_element_type=F32)
        for lp in range(len(lags)):
            v_ref[r * half:(r + 1) * half, lp * LANE:(lp + 1) * LANE] = jnp.where(
                rowg == colg, piece[:, lp * LANE:(lp + 1) * LANE], 0.0).astype(BF16)


def _s5_body(u_ref, bx_ref, cax_ref, cw_ref, pwf_ref, pwb_ref, d_ref, o_ref,
             tw, wf, wb, vf, vb, stage, xcat, sbuf, xf, xb, *, seq):
    n1 = S5_C2
    blk_rows = seq // S5_C2
    pitch = blk_rows + S5_PAD

    @pl.when(pl.program_id(1) == 0)
    def _():
        _expand_toeplitz(bx_ref, cax_ref, tw)
        sel_w = [_lane_select(6, tt) for tt in range(8)]
        for d, w_out in ((0, wf), (1, wb)):
            comp = jnp.concatenate([cw_ref[l, d, 0] for l in range(S5_L)], axis=0)
            _expand_blockdiag(comp, w_out, 0, sel_w, row_b_log2=4, d_log2=6)
        _expand_readout(cax_ref, vf, 0, [lp + 1 for lp in range(S5_L)])
        _expand_readout(cax_ref, vb, 1, [S5_L - lp for lp in range(S5_L)])

    def pieces():
        for c0 in range(8):
            for c1 in range(8):
                for l in range(S5_L):
                    yield (c0 * 8 * n1 + c1 * n1, l * LANE,
                           pl.ds(c1 * 64 + c0 * 8 + l, n1, stride=pitch))

    for c2 in range(S5_C2):
        stage[pl.ds(c2 * pitch, blk_rows), :] = u_ref[pl.ds(c2 * blk_rows, blk_rows), :]
    for r0, l0, tok in pieces():
        xcat[pl.ds(r0, n1), l0:l0 + LANE] = stage[tok, :]
    xb16 = xcat[...].astype(BF16)
    sbuf[...] = jnp.dot(xb16, wf[...], preferred_element_type=F32)
    _s5_scan(sbuf, xf, pwf_ref, False)
    sbuf[...] = jnp.dot(xb16, wb[...], preferred_element_type=F32)
    _s5_scan(sbuf, xb, pwb_ref, True)
    y = jnp.dot(xb16, tw[...], preferred_element_type=F32)
    y = y + jnp.dot(xf[...].astype(BF16), vf[...], preferred_element_type=F32)
    y = y + jnp.dot(xb[...].astype(BF16), vb[...], preferred_element_type=F32)
    dd = jnp.concatenate([d_ref[...]] * S5_L, axis=-1)
    sbuf[...] = jax.nn.gelu(y + dd * xcat[...])
    for r0, l0, tok in pieces():
        stage[tok, :] = sbuf[pl.ds(r0, n1), l0:l0 + LANE]
    for c2 in range(S5_C2):
        o_ref[pl.ds(c2 * blk_rows, blk_rows), :] = stage[pl.ds(c2 * pitch, blk_rows), :].astype(o_ref.dtype)


def _s5_branch(u, ops, d, bsz, seq):
    t = u.shape[0]
    bx, cax, cw, pw = ops
    ntile = S5_DIM // LANE
    rows = seq // S5_L
    wide = S5_L * LANE
    body = functools.partial(_s5_body, seq=seq)

    def pw_spec(di):
        return pl.BlockSpec((PW_ROWS, wide), lambda j, b: (0, di * ntile + j))

    return pl.pallas_call(
        body,
        grid=(ntile, bsz),
        in_specs=[pl.BlockSpec((seq, LANE), lambda j, b: (b, j)),
                  pl.BlockSpec((2, 1, LANE, LANE), lambda j, b: (0, j, 0, 0)),
                  pl.BlockSpec((S5_L + 1, 2, 1, LANE, LANE), lambda j, b: (0, 0, j, 0, 0)),
                  pl.BlockSpec((S5_L, 2, 1, LANE, LANE), lambda j, b: (0, 0, j, 0, 0)),
                  pw_spec(0), pw_spec(1),
                  pl.BlockSpec((1, LANE), lambda j, b: (0, j))],
        out_specs=pl.BlockSpec((seq, LANE), lambda j, b: (b, j)),
        out_shape=jax.ShapeDtypeStruct((t, S5_DIM), BF16),
        scratch_shapes=([pltpu.VMEM((wide, wide), BF16)] * 5
                        + [pltpu.VMEM((seq + S5_C2 * S5_PAD, LANE), F32)]
                        + [pltpu.VMEM((rows, wide), F32)] * 4),
        compiler_params=_cparams(("parallel", "arbitrary")),
        name="s5_branch",
    )(u, bx, cax, cw, pw, pw, d)


def _s5_prep(fwd, bwd):
    G, P, C, L = S5_GROUPS, S5_STATE, S5_GROUP, S5_L
    nt = G // 8
    a_re, a_im, log_dt, b_re, b_im, c_re, c_im = (jnp.stack([f, b]) for f, b in zip(fwd, bwd))
    lam_re = jnp.minimum(a_re, -1e-4)
    lam_im = a_im
    dt = jnp.exp(log_dt)[..., None]
    mag = jnp.exp(lam_re * dt)
    lb_re = mag * jnp.cos(lam_im * dt)
    lb_im = mag * jnp.sin(lam_im * dt)
    den = lam_re * lam_re + lam_im * lam_im
    nr = lb_re - 1.0
    f_re = (nr * lam_re + lb_im * lam_im) / den
    f_im = (lb_im * lam_re - nr * lam_im) / den
    bt_re = jnp.swapaxes(b_re, -1, -2)
    bt_im = jnp.swapaxes(b_im, -1, -2)
    bbt_re = f_re[:, :, None, :] * bt_re - f_im[:, :, None, :] * bt_im
    bbt_im = f_re[:, :, None, :] * bt_im + f_im[:, :, None, :] * bt_re

    def power(n):
        n = jnp.asarray(n, F32).reshape(-1, 1, 1, 1)
        m = jnp.exp(n * (lam_re * dt))
        ang = n * (lam_im * dt)
        return m * jnp.cos(ang), m * jnp.sin(ang)

    pr, pi = power(jnp.arange(L + 1))

    pr2 = jnp.concatenate([pr, -pi], axis=-1)[:, :, :, None, :]
    pi2 = jnp.concatenate([-pi, -pr], axis=-1)[:, :, :, None, :]
    c_ri = jnp.concatenate([c_re, c_re], axis=-1)[None]
    c_ii = jnp.concatenate([c_im, c_im], axis=-1)[None]
    cax = (c_ri * pr2 + c_ii * pi2).reshape(L + 1, 2, nt, 8 * C, 2 * P)
    bx = jnp.concatenate([bbt_re, bbt_im], axis=-1)

    prw = jnp.stack([pr[L - 1::-1, 0], pr[:L, 1]], axis=1)
    piw = jnp.stack([pi[L - 1::-1, 0], pi[:L, 1]], axis=1)
    aa = jnp.concatenate([prw, prw], axis=-1)[:, :, :, None, :]
    ab = jnp.concatenate([-piw, piw], axis=-1)[:, :, :, None, :]
    x2 = jnp.concatenate([bbt_im, bbt_re], axis=-1)[None]
    cw = (aa * bx[None] + ab * x2).astype(BF16).reshape(L, 2, nt, 8 * C, 2 * P)
    bx = bx.reshape(2, nt, 8 * C, 2 * P)

    ns = ([8.0, 64.0, 512.0, 1024.0, 2048.0] + [64.0 * (k + 1) for k in range(8)]
          + [8.0 * k for k in range(8)] + [0.0] * (PW_ROWS - 21))
    qr, qi = power(ns)
    pw = jnp.concatenate([qr.reshape(PW_ROWS, 2, nt, 8 * P), qi.reshape(PW_ROWS, 2, nt, 8 * P)],
                         axis=-1).reshape(PW_ROWS, 2 * nt * 2 * 8 * P)
    return bx, cax, cw, pw


GLU_COLS = 256


def _glu_mm_body(a_ref, w_ref, g_ref, o_ref):
    a = a_ref[...]
    n = o_ref.shape[1]
    for c0 in range(0, n, GLU_COLS):
        z1 = jnp.dot(a, w_ref[:, c0:c0 + GLU_COLS], preferred_element_type=F32)
        z2 = jnp.dot(a, w_ref[:, n + c0:n + c0 + GLU_COLS], preferred_element_type=F32)
        gate = _silu(g_ref[:, c0:c0 + GLU_COLS].astype(F32))
        o_ref[:, c0:c0 + GLU_COLS] = (z1 * jax.nn.sigmoid(z2) * gate).astype(o_ref.dtype)


def _glu_matmul(a, w, h2, tm=1024):
    t, k = a.shape
    n = w.shape[1] // 2
    return pl.pallas_call(
        _glu_mm_body,
        grid=(t // tm,),
        in_specs=[pl.BlockSpec((tm, k), lambda i: (i, 0)),
                  pl.BlockSpec((k, 2 * n), lambda i: (0, 0)),
                  pl.BlockSpec((tm, n), lambda i: (i, O_GATE // n))],
        out_specs=pl.BlockSpec((tm, n), lambda i: (i, 0)),
        out_shape=jax.ShapeDtypeStruct((t, n), BF16),
        compiler_params=_cparams(("parallel",)),
        name="s5_glu",
    )(a, w, h2)


def _even_in_weight(w_in):
    w = w_in.astype(BF16)
    k = w.shape[0]
    k_rope = w[:, E_HEAD:E_HEAD + QK_ROPE]
    z = jnp.zeros((k, LANE - QK_ROPE), BF16)
    return w, jnp.concatenate([w[:, E_HEAD + QK_ROPE:], k_rope, z], axis=1)


def _uq_weight(w_uq):
    k = w_uq.shape[0]
    w = w_uq.reshape(k, MLA_HEADS, QK_NOPE + QK_ROPE)
    nope = w[:, :, :QK_NOPE].reshape(k, MLA_HEADS * QK_NOPE)
    rope = w[:, :, QK_NOPE:].reshape(k, MLA_HEADS * QK_ROPE)
    return jnp.concatenate([nope, rope], axis=1).astype(BF16)


def _ukv_weight(w_ukv):
    k = w_ukv.shape[0]
    w = w_ukv.reshape(k, MLA_HEADS, QK_NOPE + V_DIM)
    kn = w[:, :, :QK_NOPE].reshape(k, MLA_HEADS * QK_NOPE)
    vv = w[:, :, QK_NOPE:].reshape(k, MLA_HEADS * V_DIM)
    return jnp.concatenate([kn, vv], axis=1).astype(BF16)


def _rope_tables(positions):
    inv_freq = ROPE_THETA ** (-jnp.arange(0, QK_ROPE, 2, dtype=F32) / QK_ROPE)
    half = QK_ROPE // 2
    inv4 = jnp.tile(inv_freq, LANE // half)
    sign = jnp.tile(jnp.concatenate([-jnp.ones((half,), F32), jnp.ones((half,), F32)]),
                    LANE // QK_ROPE)
    ang = positions.astype(F32)[..., None] * inv4
    return jnp.cos(ang), jnp.sin(ang) * sign


def _row(v):
    return v.reshape(1, -1)


def _even_layer(x2, mem2, cos128, sin128, bsz, seq, w_in, conv_w, conv_b, conv_ln_g, conv_ln_b,
                q_norm, w_uq, kv_norm, w_ukv, w_mem_kv, w_out, ln_g, ln_b):
    taps = jnp.broadcast_to(conv_w.reshape(CONV_WIDTH, 1, CONV_DIM), (CONV_WIDTH, SUBLANE, CONV_DIM))
    h, a_out = _in_proj_conv(x2, *_even_in_weight(w_in), taps, _row(conv_b), _row(conv_ln_g),
                             _row(conv_ln_b), seq)
    q, k, v = _mla_proj(h, cos128, sin128, _row(q_norm), _row(kv_norm), _uq_weight(w_uq),
                        _ukv_weight(w_ukv), bsz, seq)
    b_out = _mla_attention(q, k, v, h)
    (memkv,) = _proj(mem2, w_mem_kv.astype(BF16), [(2 * MEM_DIM, BF16)], 512, "even_mem_kv")
    memkv = memkv.reshape(bsz, -1, 2 * MEM_DIM)
    return _out_ln([a_out, b_out], h, memkv, E_MEM_Q, E_MEM_GATE, seq, w_out.astype(BF16), x2,
                   _row(ln_g), _row(ln_b), name="even_out_ln")


def _odd_layer(x2, mem2, bsz, seq, w_in, s5_fwd, s5_bwd, s5_d, w_glu, w_mem_kv, w_out, ln_g, ln_b):
    n_in = w_in.shape[1]
    u, h2 = _proj(x2, w_in.astype(BF16), [(O_SPLIT, F32), (n_in - O_SPLIT, BF16)], 1024,
                  "odd_in_proj")
    g = _s5_branch(u, _s5_prep(s5_fwd, s5_bwd), _row(s5_d), bsz, seq)
    c_out = _glu_matmul(g, w_glu.astype(BF16), h2)
    (memkv,) = _proj(mem2, w_mem_kv.astype(BF16), [(2 * MEM_DIM, BF16)], 512, "odd_mem_kv")
    memkv = memkv.reshape(bsz, -1, 2 * MEM_DIM)
    return _out_ln([c_out], h2, memkv, O_MEM_Q, O_MEM_GATE, seq, w_out.astype(BF16), x2,
                   _row(ln_g), _row(ln_b), name="odd_out_ln")


def kernel(x, mem, positions, e_w_in, e_conv_w, e_conv_b, e_conv_ln_g, e_conv_ln_b, e_q_norm, e_w_uq, e_kv_norm, e_w_ukv, e_mem_kv, e_w_out, e_ln_g, e_ln_b, o_w_in, o_a_re_f, o_a_im_f, o_log_dt_f, o_b_re_f, o_b_im_f, o_c_re_f, o_c_im_f, o_a_re_b, o_a_im_b, o_log_dt_b, o_b_re_b, o_b_im_b, o_c_re_b, o_c_im_b, o_d, o_w_glu, o_mem_kv, o_w_out, o_ln_g, o_ln_b):
    bsz, seq, d = x.shape
    cos128, sin128 = _rope_tables(positions)
    x2 = x.reshape(bsz * seq, d)
    mem2 = mem.reshape(-1, d)
    h = _even_layer(x2, mem2, cos128, sin128, bsz, seq, e_w_in[0], e_conv_w[0], e_conv_b[0],
                    e_conv_ln_g[0], e_conv_ln_b[0], e_q_norm[0], e_w_uq[0], e_kv_norm[0],
                    e_w_ukv[0], e_mem_kv[0], e_w_out[0], e_ln_g[0], e_ln_b[0])
    s5_fwd = (o_a_re_f[0], o_a_im_f[0], o_log_dt_f[0], o_b_re_f[0], o_b_im_f[0], o_c_re_f[0], o_c_im_f[0])
    s5_bwd = (o_a_re_b[0], o_a_im_b[0], o_log_dt_b[0], o_b_re_b[0], o_b_im_b[0], o_c_re_b[0], o_c_im_b[0])
    h = _odd_layer(h, mem2, bsz, seq, o_w_in[0], s5_fwd, s5_bwd, o_d[0], o_w_glu[0], o_mem_kv[0],
                   o_w_out[0], o_ln_g[0], o_ln_b[0])
    return h.reshape(bsz, seq, d)
```

```python
import functools
import math

import jax
import jax.numpy as jnp
from jax import lax
from jax.experimental import pallas as pl
from jax.experimental.pallas import tpu as pltpu

F32 = jnp.float32
BF16 = jnp.bfloat16

D_MODEL = 1024
CONV_DIM = 1024
CONV_WIDTH = 31
CONV_PAD = 15
MLA_HEADS = 8
QK_NOPE = 128
QK_ROPE = 64
V_DIM = 128
Q_LORA = 768
KV_LORA = 256
ROPE_THETA = 10000.0
MEM_HEADS = 4
MEM_HEAD_DIM = 128
MEM_DIM = 512
S5_DIM = 1024
S5_GROUP = 16
S5_GROUPS = 64
S5_STATE = 64
LN_EPS = 1e-5
RMS_EPS = 1e-6
DEPTH = 2
ALPHA = (2 * DEPTH) ** 0.25

LANE = 128
SUBLANE = 8
VMEM_LIMIT = 56 * 1024 * 1024

E_CONV = 3 * CONV_DIM
E_CQ = 0
E_CKV = 768
E_MLA_GATE = 1024
E_MEM_Q = 2048
E_MEM_GATE = 2560
E_KROPE = 3072
E_NREST = 3200
E_HEAD = E_CONV + Q_LORA + KV_LORA
O_SPLIT = 1024
O_GATE = 0
O_MEM_Q = 1024
O_MEM_GATE = 1536

S5_L = 8


def _cparams(sem):
    return pltpu.CompilerParams(dimension_semantics=sem, vmem_limit_bytes=VMEM_LIMIT)


def _silu(x):
    return x * jax.nn.sigmoid(x)


def _proj_body(a_ref, w_ref, *o_refs, bounds):
    y = jnp.dot(a_ref[...].astype(BF16), w_ref[...], preferred_element_type=F32)
    for o_ref, (lo, hi) in zip(o_refs, bounds):
        o_ref[...] = y[:, lo:hi].astype(o_ref.dtype)


def _proj(a, w, outs, tm, name):
    m, k = a.shape
    n = w.shape[1]
    tm = min(tm, m)
    assert m % tm == 0 and sum(width for width, _ in outs) == n
    bounds, lo = [], 0
    for width, _ in outs:
        bounds.append((lo, lo + width))
        lo += width
    res = pl.pallas_call(
        functools.partial(_proj_body, bounds=tuple(bounds)),
        grid=(m // tm,),
        in_specs=[pl.BlockSpec((tm, k), lambda i: (i, 0)),
                  pl.BlockSpec((k, n), lambda i: (0, 0))],
        out_specs=[pl.BlockSpec((tm, width), lambda i: (i, 0)) for width, _ in outs],
        out_shape=[jax.ShapeDtypeStruct((m, width), dt) for width, dt in outs],
        compiler_params=_cparams(("parallel",)),
        name=name,
    )(a, w)
    return res


CONV_HALO = 16
CONV_RC = 32


def _in_conv_body(x_ref, wa_ref, wb_ref, cw, cb, lg, lb, h_ref, o_ref, xb, slab, cur, gcur, gate, tail,
                  *, ts, tiles_per_seq):
    i = pl.program_id(0)

    @pl.when(i == 0)
    def _():
        cur[...] = jnp.zeros_like(cur)
        gcur[...] = jnp.zeros_like(gcur)
        tail[...] = jnp.zeros_like(tail)

    xb[...] = x_ref[...].astype(BF16)
    na = wa_ref.shape[1]
    n = na + wb_ref.shape[1]

    def proj(lo, hi):
        assert hi <= na or lo >= na
        w = wa_ref[:, lo:hi] if hi <= na else wb_ref[:, lo - na:hi - na]
        return jnp.dot(xb[...], w, preferred_element_type=F32)

    glu_new = proj(0, CONV_DIM) * jax.nn.sigmoid(proj(CONV_DIM, 2 * CONV_DIM))

    j = i - 1
    first = (j % tiles_per_seq) == 0
    last = (j % tiles_per_seq) == tiles_per_seq - 1
    slab[0, pl.ds(CONV_HALO, ts), :] = cur[...]
    slab[0, pl.ds(0, CONV_HALO), :] = jnp.where(first, 0.0, tail[...])
    slab[0, pl.ds(CONV_HALO + ts, CONV_HALO), :] = jnp.where(last, 0.0, glu_new[0:CONV_HALO, :])
    gate[...] = gcur[...]
    tail[...] = cur[pl.ds(ts - CONV_HALO, CONV_HALO), :]
    cur[...] = glu_new

    rows = ts + 2 * CONV_HALO
    for lt in range(CONV_DIM // LANE):
        cols = slice(lt * LANE, (lt + 1) * LANE)
        base = slab[0, :, cols]
        for r in range(1, SUBLANE):
            slab[r, :, cols] = pltpu.roll(base, rows - r, 0)

    nchunks = ts // CONV_RC
    step_cols = (n - 2 * CONV_DIM) // nchunks // LANE * LANE
    off = CONV_HALO - CONV_PAD
    for rc in range(nchunks):
        lo = 2 * CONV_DIM + rc * step_cols
        hi = n if rc == nchunks - 1 else lo + step_cols
        yp = proj(lo, hi)
        if hi <= E_CONV:
            gcur[:, lo - 2 * CONV_DIM:hi - 2 * CONV_DIM] = yp.astype(gcur.dtype)
        else:
            assert lo >= E_CONV
            h_ref[:, lo - E_CONV:hi - E_CONV] = yp.astype(h_ref.dtype)

        base = rc * CONV_RC
        acc = jnp.zeros((CONV_RC, CONV_DIM), F32)
        for k in range(CONV_WIDTH):
            r = (off + k) % SUBLANE
            wk = jnp.concatenate([cw[k]] * (CONV_RC // SUBLANE), axis=0)
            acc = acc + slab[r, pl.ds(base + off + k - r, CONV_RC), :] * wk
        dw = acc + cb[...]
        mu = jnp.mean(dw, axis=-1, keepdims=True)
        xc = dw - mu
        var = jnp.mean(xc * xc, axis=-1, keepdims=True)
        yn = xc * lax.rsqrt(var + LN_EPS) * lg[...] + lb[...]
        res = _silu(yn) * _silu(gate[pl.ds(base, CONV_RC), :].astype(F32))
        o_ref[pl.ds(base, CONV_RC), :] = res.astype(o_ref.dtype)


def _in_proj_conv(x, w_full, w_tail, taps, conv_b, ln_g, ln_b, seq, ts=256):
    t, k = x.shape
    n = E_HEAD + w_tail.shape[1]
    nt = t // ts
    body = functools.partial(_in_conv_body, ts=ts, tiles_per_seq=seq // ts)
    vec = pl.BlockSpec((1, CONV_DIM), lambda i: (0, 0))
    return pl.pallas_call(
        body,
        grid=(nt + 1,),
        in_specs=[
            pl.BlockSpec((ts, k), lambda i: (jnp.minimum(i, nt - 1), 0)),
            pl.BlockSpec((k, E_HEAD), lambda i: (0, 0)),
            pl.BlockSpec(w_tail.shape, lambda i: (0, 0)),
            pl.BlockSpec((CONV_WIDTH, SUBLANE, CONV_DIM), lambda i: (0, 0, 0)),
            vec, vec, vec,
        ],
        out_specs=[pl.BlockSpec((ts, n - E_CONV), lambda i: (jnp.minimum(i, nt - 1), 0)),
                   pl.BlockSpec((ts, CONV_DIM), lambda i: (jnp.maximum(i - 1, 0), 0))],
        out_shape=[jax.ShapeDtypeStruct((t, n - E_CONV), BF16),
                   jax.ShapeDtypeStruct((t, CONV_DIM), BF16)],
        scratch_shapes=[pltpu.VMEM((ts, k), BF16),
                        pltpu.VMEM((SUBLANE, ts + 2 * CONV_HALO, CONV_DIM), F32),
                        pltpu.VMEM((ts, CONV_DIM), F32),
                        pltpu.VMEM((ts, CONV_DIM), BF16),
                        pltpu.VMEM((ts, CONV_DIM), BF16),
                        pltpu.VMEM((CONV_HALO, CONV_DIM), F32)],
        compiler_params=_cparams(("arbitrary",)),
        name="in_proj_conv",
    )(x, w_full, w_tail, taps, conv_b, ln_g, ln_b)


def _rms(x, g):
    ms = jnp.mean(x * x, axis=-1, keepdims=True)
    return x * lax.rsqrt(ms + RMS_EPS) * g


MLA_PROJ_CHUNKS = 2
ROPE_PACK = LANE // (QK_ROPE // 2)


def _expand_rope_table(tbl_ref, r0, rc, out_ref, sign):
    half = QK_ROPE // 2
    tbl = tbl_ref[0, pl.ds(r0 // ROPE_PACK, rc // ROPE_PACK), :]
    lane = lax.broadcasted_iota(jnp.int32, tbl.shape, 1)
    for q in range(ROPE_PACK):
        seg = tbl if q == 0 else pltpu.roll(tbl, LANE - q * half, 1)
        m = jnp.where(lane < half, seg, 0.0)
        m = m + pltpu.roll(m, half, 1)
        m = m + pltpu.roll(m, 2 * half, 1)
        if sign is not None:
            m = m * sign
        out_ref[pl.ds(q, rc // ROPE_PACK, stride=ROPE_PACK), :] = m
    return out_ref[...]


def _rope(x, c, s):
    half = QK_ROPE // 2
    lane = lax.broadcasted_iota(jnp.int32, x.shape, 1)
    swapped = jnp.where((lane & (QK_ROPE - 1)) < half,
                        pltpu.roll(x, LANE - half, 1),
                        pltpu.roll(x, half, 1))
    return x * c + swapped * s


def _mla_proj_body(cq, ckv, kr, cos, sin, sign, qn_g, kvn_g, wq, wkv, q_ref, k_ref, v_ref, cexp, sexp,
                   *, scale):
    nh = MLA_HEADS * QK_NOPE
    nr = MLA_HEADS * QK_ROPE
    rc = cq.shape[0] // MLA_PROJ_CHUNKS
    for r0 in range(0, cq.shape[0], rc):
        rows = pl.ds(r0, rc)
        c = _expand_rope_table(cos, r0, rc, cexp, None)
        s = _expand_rope_table(sin, r0, rc, sexp, sign[...])
        nq = _rms(cq[rows, :].astype(F32), qn_g[...]).astype(BF16)
        qf = jnp.dot(nq, wq[...], preferred_element_type=F32)
        qr = [_rope(qf[:, nh + p * LANE:nh + (p + 1) * LANE], c, s) for p in range(nr // LANE)]
        nkv = _rms(ckv[rows, :].astype(F32), kvn_g[...]).astype(BF16)
        kvf = jnp.dot(nkv, wkv[...], preferred_element_type=F32)
        kr_even = _rope(kr[rows, :].astype(F32), c, s)
        kr_odd = pltpu.roll(kr_even, QK_ROPE, 1)
        for h in range(MLA_HEADS):
            q_ref[0, h, rows, 0:QK_NOPE] = (qf[:, h * QK_NOPE:(h + 1) * QK_NOPE] * scale).astype(BF16)
            q_ref[0, h, rows, QK_NOPE:2 * QK_NOPE] = (qr[h // 2] * scale).astype(BF16)
            k_ref[0, h, rows, 0:QK_NOPE] = kvf[:, h * QK_NOPE:(h + 1) * QK_NOPE].astype(BF16)
            k_ref[0, h, rows, QK_NOPE:2 * QK_NOPE] = (kr_even if h % 2 == 0 else kr_odd).astype(BF16)
            v_ref[0, h, rows, :] = kvf[:, nh + h * V_DIM:nh + (h + 1) * V_DIM].astype(BF16)


def _mla_proj(h, cos128, sin128, sign, q_norm, kv_norm, wq, wkv, bsz, seq, tm=1024):
    nt = seq // tm
    scale = (QK_NOPE + QK_ROPE) ** -0.5 * math.log2(math.e)
    body = functools.partial(_mla_proj_body, scale=scale)
    hd = 2 * QK_NOPE
    return pl.pallas_call(
        body,
        grid=(bsz, nt),
        in_specs=[
            pl.BlockSpec((tm, Q_LORA), lambda b, i: (b * nt + i, E_CQ // Q_LORA)),
            pl.BlockSpec((tm, KV_LORA), lambda b, i: (b * nt + i, E_CKV // KV_LORA)),
            pl.BlockSpec((tm, LANE), lambda b, i: (b * nt + i, E_KROPE // LANE)),
            pl.BlockSpec((1, tm // ROPE_PACK, LANE), lambda b, i: (b, i, 0)),
            pl.BlockSpec((1, tm // ROPE_PACK, LANE), lambda b, i: (b, i, 0)),
            pl.BlockSpec((1, LANE), lambda b, i: (0, 0)),
            pl.BlockSpec((1, Q_LORA), lambda b, i: (0, 0)),
            pl.BlockSpec((1, KV_LORA), lambda b, i: (0, 0)),
            pl.BlockSpec(wq.shape, lambda b, i: (0, 0)),
            pl.BlockSpec(wkv.shape, lambda b, i: (0, 0)),
        ],
        out_specs=[
            pl.BlockSpec((1, MLA_HEADS, tm, hd), lambda b, i: (b, 0, i, 0)),
            pl.BlockSpec((1, MLA_HEADS, tm, hd), lambda b, i: (b, 0, i, 0)),
            pl.BlockSpec((1, MLA_HEADS, tm, V_DIM), lambda b, i: (b, 0, i, 0)),
        ],
        out_shape=[
            jax.ShapeDtypeStruct((bsz, MLA_HEADS, seq, hd), BF16),
            jax.ShapeDtypeStruct((bsz, MLA_HEADS, seq, hd), BF16),
            jax.ShapeDtypeStruct((bsz, MLA_HEADS, seq, V_DIM), BF16),
        ],
        scratch_shapes=[pltpu.VMEM((tm // MLA_PROJ_CHUNKS, LANE), F32)] * 2,
        compiler_params=_cparams(("parallel", "parallel")),
        name="mla_proj",
    )(h, h, h, cos128, sin128, sign, q_norm, kv_norm, wq, wkv)


def _flash_body(q_ref, k_ref, v_ref, g_ref, o_ref, *, tk, nk, unroll):
    q = q_ref[0, 0]
    tq = q.shape[0]

    ones = jnp.ones((tk, V_DIM), BF16)

    def step(c, carry):
        m, acc = carry
        start = pl.multiple_of(c * tk, tk)
        ks = k_ref[0, 0, pl.ds(start, tk), :]
        vs = jnp.concatenate([v_ref[0, 0, pl.ds(start, tk), :], ones], axis=1)
        s = lax.dot_general(q, ks, (((1,), (1,)), ((), ())), preferred_element_type=F32)
        m_new = jnp.maximum(m, jnp.max(s, axis=-1, keepdims=True))
        alpha = jnp.exp2(m - m_new)
        p = jnp.exp2(s - m_new)
        acc = alpha * acc + jnp.dot(p.astype(BF16), vs, preferred_element_type=F32)
        return m_new, acc

    m0 = jnp.full((tq, 1), -jnp.inf, F32)
    a0 = jnp.zeros((tq, 2 * V_DIM), F32)
    m, acc = lax.fori_loop(0, nk, step, (m0, a0), unroll=unroll)
    o_ref[...] = (acc[:, :V_DIM] / acc[:, V_DIM:] * _silu(g_ref[...].astype(F32))).astype(o_ref.dtype)


def _mla_attention(q, k, v, h, tq=2048, tk=256, unroll=True):
    bsz, heads, seq, hd = q.shape
    nq = seq // tq
    body = functools.partial(_flash_body, tk=tk, nk=seq // tk, unroll=unroll)
    gcol = E_MLA_GATE // V_DIM
    return pl.pallas_call(
        body,
        grid=(bsz, heads, nq),
        in_specs=[
            pl.BlockSpec((1, 1, tq, hd), lambda b, hh, i: (b, hh, i, 0)),
            pl.BlockSpec((1, 1, seq, hd), lambda b, hh, i: (b, hh, 0, 0)),
            pl.BlockSpec((1, 1, seq, V_DIM), lambda b, hh, i: (b, hh, 0, 0)),
            pl.BlockSpec((tq, V_DIM), lambda b, hh, i: (b * nq + i, gcol + hh)),
        ],
        out_specs=pl.BlockSpec((tq, V_DIM), lambda b, hh, i: (b * nq + i, hh)),
        out_shape=jax.ShapeDtypeStruct((bsz * seq, heads * V_DIM), BF16),
        compiler_params=_cparams(("parallel", "parallel", "arbitrary")),
        name="mla_attention",
    )(q, k, v, h)


OUT_LN_CHUNKS = 4


def _mem_attn(q_ref, g_ref, kv_ref, rows, scale):
    ones = jnp.ones((kv_ref.shape[1], MEM_HEAD_DIM), BF16)
    outs = []
    for hh in range(MEM_HEADS):
        lo, hi = hh * MEM_HEAD_DIM, (hh + 1) * MEM_HEAD_DIM
        kk = kv_ref[0, :, lo:hi]
        vv = jnp.concatenate([kv_ref[0, :, MEM_DIM + lo:MEM_DIM + hi], ones], axis=1)
        s = lax.dot_general(q_ref[rows, lo:hi], kk, (((1,), (1,)), ((), ())),
                            preferred_element_type=F32) * scale
        m = jnp.max(s, axis=-1, keepdims=True)
        p = jnp.exp2(s - m)
        o = jnp.dot(p.astype(BF16), vv, preferred_element_type=F32)
        o = o[:, :MEM_HEAD_DIM] / o[:, MEM_HEAD_DIM:]
        outs.append((o * _silu(g_ref[rows, lo:hi].astype(F32))).astype(BF16))
    return jnp.concatenate(outs, axis=1)


def _out_ln_body(*refs, nparts, scale):
    parts = refs[:nparts]
    q_ref, mg_ref, kv_ref, w_ref, x_ref, g_ref, b_ref, o_ref = refs[nparts:]
    rc = o_ref.shape[0] // OUT_LN_CHUNKS
    for r0 in range(0, o_ref.shape[0], rc):
        rows = pl.ds(r0, rc)
        m_out = _mem_attn(q_ref, mg_ref, kv_ref, rows, scale)
        cat = jnp.concatenate([p[rows, :] for p in parts] + [m_out], axis=1)
        y = jnp.dot(cat, w_ref[...], preferred_element_type=F32)
        z = ALPHA * x_ref[rows, :] + y
        mu = jnp.mean(z, axis=-1, keepdims=True)
        zc = z - mu
        var = jnp.mean(zc * zc, axis=-1, keepdims=True)
        o_ref[rows, :] = zc * lax.rsqrt(var + LN_EPS) * g_ref[...] + b_ref[...]


def _out_ln(parts, h, memkv, q_off, g_off, seq, w, x, g, b, tm=1024, name="out_ln"):
    t = x.shape[0]
    n = len(parts)
    nt = seq // tm
    mlen = memkv.shape[1]
    body = functools.partial(_out_ln_body, nparts=n, scale=MEM_HEAD_DIM ** -0.5 * math.log2(math.e))
    assert sum(p.shape[1] for p in parts) + MEM_DIM == w.shape[0]
    in_specs = [pl.BlockSpec((tm, p.shape[1]), lambda i: (i, 0)) for p in parts]
    in_specs += [pl.BlockSpec((tm, MEM_DIM), lambda i: (i, q_off // MEM_DIM)),
                 pl.BlockSpec((tm, MEM_DIM), lambda i: (i, g_off // MEM_DIM)),
                 pl.BlockSpec((1, mlen, 2 * MEM_DIM), lambda i: (i // nt, 0, 0)),
                 pl.BlockSpec(w.shape, lambda i: (0, 0)),
                 pl.BlockSpec((tm, D_MODEL), lambda i: (i, 0)),
                 pl.BlockSpec((1, D_MODEL), lambda i: (0, 0)),
                 pl.BlockSpec((1, D_MODEL), lambda i: (0, 0))]
    return pl.pallas_call(
        body,
        grid=(t // tm,),
        in_specs=in_specs,
        out_specs=pl.BlockSpec((tm, D_MODEL), lambda i: (i, 0)),
        out_shape=jax.ShapeDtypeStruct((t, D_MODEL), F32),
        compiler_params=_cparams(("parallel",)),
        name=name,
    )(*parts, h, h, memkv, w, x, g, b)


S5_C2 = 8
PW_A8, PW_A64, PW_A512, PW_A1024, PW_A2048 = 0, 1, 2, 3, 4
PW_A64K = 5
PW_A8K = 13
PW_ROWS = 24
S5_PAD = 8


def _cmul(pr, pi, xr, xi):
    return pr * xr - pi * xi, pr * xi + pi * xr


def _s5_scan(s_ref, x_ref, pw_ref, rev):
    half = 512
    n0 = 8 * S5_C2
    n1 = S5_C2

    def pw(row):
        v = pw_ref[pl.ds(row, 1), :]
        return v[:, :half], v[:, half:]

    def blk0(k):
        return (7 - k if rev else k) * n0

    def blk1(k):
        return (7 - k if rev else k) * n1

    def ld(ref, start, n):
        v = ref[pl.ds(start, n), :]
        return v[:, :half], v[:, half:]

    def st(ref, start, n, re, im):
        ref[pl.ds(start, n), 0:half] = re
        ref[pl.ds(start, n), half:2 * half] = im

    a8r, a8i = pw(PW_A8)
    for k in range(1, 8):
        pr_, pi_ = ld(s_ref, blk0(k - 1), n0)
        cr, ci = ld(s_ref, blk0(k), n0)
        mr, mi = _cmul(a8r, a8i, pr_, pi_)
        st(s_ref, blk0(k), n0, cr + mr, ci + mi)
    g0 = blk0(7)
    a64r, a64i = pw(PW_A64)
    for k in range(1, 8):
        pr_, pi_ = ld(s_ref, g0 + blk1(k - 1), n1)
        cr, ci = ld(s_ref, g0 + blk1(k), n1)
        mr, mi = _cmul(a64r, a64i, pr_, pi_)
        st(s_ref, g0 + blk1(k), n1, cr + mr, ci + mi)
    hr, hi = ld(s_ref, g0 + blk1(7), n1)
    c2 = lax.broadcasted_iota(jnp.int32, (n1, half), 0)
    if rev:
        c2 = (S5_C2 - 1) - c2
    for d, row in ((1, PW_A512), (2, PW_A1024), (4, PW_A2048)):
        ar, ai = pw(row)
        sh = (n1 - d) if rev else d
        sr = pltpu.roll(hr, sh, 0)
        si = pltpu.roll(hi, sh, 0)
        mr, mi = _cmul(ar, ai, sr, si)
        keep = c2 >= d
        hr = hr + jnp.where(keep, mr, 0.0)
        hi = hi + jnp.where(keep, mi, 0.0)
    sh1 = (n1 - 1) if rev else 1
    p3r = jnp.where(c2 >= 1, pltpu.roll(hr, sh1, 0), 0.0)
    p3i = jnp.where(c2 >= 1, pltpu.roll(hi, sh1, 0), 0.0)
    for k1 in range(8):
        if k1 == 0:
            er, ei = p3r, p3i
        else:
            ar, ai = pw(PW_A64K + k1 - 1)
            mr, mi = _cmul(ar, ai, p3r, p3i)
            qr, qi = ld(s_ref, g0 + blk1(k1 - 1), n1)
            er, ei = qr + mr, qi + mi
        for k0 in range(8):
            dst = blk0(k0) + blk1(k1)
            if k0 == 0:
                st(x_ref, dst, n1, er, ei)
            else:
                ar, ai = pw(PW_A8K + k0)
                mr, mi = _cmul(ar, ai, er, ei)
                qr, qi = ld(s_ref, blk0(k0 - 1) + blk1(k1), n1)
                st(x_ref, dst, n1, qr + mr, qi + mi)


def _lane_select(d_log2, tt):
    kk = lax.broadcasted_iota(jnp.int32, (LANE, LANE), 0)
    col = tt * LANE + lax.broadcasted_iota(jnp.int32, (LANE, LANE), 1)
    src = lax.shift_left(lax.shift_right_logical(col, d_log2 + 3), d_log2) + (col & ((1 << d_log2) - 1))
    return jnp.where(kk == src, 1.0, 0.0).astype(BF16)


def _expand_blockdiag(comp, m_ref, row0, sels, *, row_b_log2, d_log2):
    rows = comp.shape[0]
    rowg = lax.shift_right_logical(lax.broadcasted_iota(jnp.int32, (rows, LANE), 0), row_b_log2) & 7
    coln = lax.broadcasted_iota(jnp.int32, (rows, LANE), 1)
    for tt in range(8):
        piece = jnp.dot(comp, sels[tt], preferred_element_type=F32)
        colg = lax.shift_right_logical(tt * LANE + coln, d_log2) & 7
        m_ref[pl.ds(row0, rows), tt * LANE:(tt + 1) * LANE] = jnp.where(
            rowg == colg, piece, 0.0).astype(BF16)


_NT = (((1,), (1,)), ((), ()))


def _expand_toeplitz(bx_ref, cax_ref, tw):
    rowg = lax.shift_right_logical(lax.broadcasted_iota(jnp.int32, (LANE, LANE), 0), 4)
    colg = lax.shift_right_logical(lax.broadcasted_iota(jnp.int32, (LANE, LANE), 1), 4)
    keep = rowg == colg

    def kern(n, d):
        return lax.dot_general(bx_ref[d, 0], cax_ref[n, d, 0], _NT,
                               precision=lax.Precision.HIGHEST, preferred_element_type=F32)

    for lag in range(-(S5_L - 1), S5_L):
        if lag == 0:
            k = kern(0, 0) + kern(0, 1)
        else:
            k = kern(lag, 0) if lag > 0 else kern(-lag, 1)
        blk = jnp.where(keep, k, 0.0).astype(BF16)
        for l in range(S5_L):
            lp = l + lag
            if 0 <= lp < S5_L:
                tw[l * LANE:(l + 1) * LANE, lp * LANE:(lp + 1) * LANE] = blk


def _expand_readout(cax_ref, v_ref, d, lags):
    half = 8 * S5_STATE
    row = lax.broadcasted_iota(jnp.int32, (half, LANE), 0)
    lane = lax.broadcasted_iota(jnp.int32, (half, LANE), 1)
    rowg = lax.shift_right_logical(row, 6)
    colg = lax.shift_right_logical(lane, 4)
    for r in range(2):
        pick = jnp.where(((row & (S5_STATE - 1)) + r * S5_STATE) == lane, 1.0, 0.0).astype(BF16)
        for lp, n in enumerate(lags):
            piece = lax.dot_general(pick, cax_ref[n, d, 0].astype(BF16), _NT, preferred_element_type=F32)
            v_ref[r * half:(r + 1) * half, lp * LANE:(lp + 1) * LANE] = jnp.where(
                rowg == colg, piece, 0.0).astype(BF16)


def _s5_body(u_ref, bx_ref, cax_ref, cw_ref, pwf_ref, pwb_ref, d_ref, o_ref,
             tw, wf, wb, vf, vb, stage, xcat, sbuf, xf, xb, *, seq):
    n1 = S5_C2
    blk_rows = seq // S5_C2
    pitch = blk_rows + S5_PAD

    @pl.when(pl.program_id(1) == 0)
    def _():
        _expand_toeplitz(bx_ref, cax_ref, tw)
        sel_w = [_lane_select(6, tt) for tt in range(8)]
        for l in range(S5_L):
            _expand_blockdiag(cw_ref[l, 0, 0], wf, l * LANE, sel_w, row_b_log2=4, d_log2=6)
            _expand_blockdiag(cw_ref[l, 1, 0], wb, l * LANE, sel_w, row_b_log2=4, d_log2=6)
        _expand_readout(cax_ref, vf, 0, [lp + 1 for lp in range(S5_L)])
        _expand_readout(cax_ref, vb, 1, [S5_L - lp for lp in range(S5_L)])

    def pieces():
        for c0 in range(8):
            for c1 in range(8):
                for l in range(S5_L):
                    yield (c0 * 8 * n1 + c1 * n1, l * LANE,
                           pl.ds(c1 * 64 + c0 * 8 + l, n1, stride=pitch))

    for c2 in range(S5_C2):
        stage[pl.ds(c2 * pitch, blk_rows), :] = u_ref[pl.ds(c2 * blk_rows, blk_rows), :]
    for r0, l0, tok in pieces():
        xcat[pl.ds(r0, n1), l0:l0 + LANE] = stage[tok, :]
    xb16 = xcat[...].astype(BF16)
    sbuf[...] = jnp.dot(xb16, wf[...], preferred_element_type=F32)
    _s5_scan(sbuf, xf, pwf_ref, False)
    sbuf[...] = jnp.dot(xb16, wb[...], preferred_element_type=F32)
    _s5_scan(sbuf, xb, pwb_ref, True)
    y = jnp.dot(xb16, tw[...], preferred_element_type=F32)
    y = y + jnp.dot(xf[...].astype(BF16), vf[...], preferred_element_type=F32)
    y = y + jnp.dot(xb[...].astype(BF16), vb[...], preferred_element_type=F32)
    dd = jnp.concatenate([d_ref[...]] * S5_L, axis=-1)
    sbuf[...] = jax.nn.gelu(y + dd * xcat[...])
    for r0, l0, tok in pieces():
        stage[tok, :] = sbuf[pl.ds(r0, n1), l0:l0 + LANE]
    for c2 in range(S5_C2):
        o_ref[pl.ds(c2 * blk_rows, blk_rows), :] = stage[pl.ds(c2 * pitch, blk_rows), :].astype(o_ref.dtype)


def _s5_branch(u, ops, d, bsz, seq):
    t = u.shape[0]
    bx, cax, cw, pw = ops
    ntile = S5_DIM // LANE
    rows = seq // S5_L
    wide = S5_L * LANE
    body = functools.partial(_s5_body, seq=seq)

    def pw_spec(di):
        return pl.BlockSpec((PW_ROWS, wide), lambda j, b: (0, di * ntile + j))

    return pl.pallas_call(
        body,
        grid=(ntile, bsz),
        in_specs=[pl.BlockSpec((seq, LANE), lambda j, b: (b, j)),
                  pl.BlockSpec((2, 1, LANE, LANE), lambda j, b: (0, j, 0, 0)),
                  pl.BlockSpec((S5_L + 1, 2, 1, LANE, LANE), lambda j, b: (0, 0, j, 0, 0)),
                  pl.BlockSpec((S5_L, 2, 1, LANE, LANE), lambda j, b: (0, 0, j, 0, 0)),
                  pw_spec(0), pw_spec(1),
                  pl.BlockSpec((1, LANE), lambda j, b: (0, j))],
        out_specs=pl.BlockSpec((seq, LANE), lambda j, b: (b, j)),
        out_shape=jax.ShapeDtypeStruct((t, S5_DIM), BF16),
        scratch_shapes=([pltpu.VMEM((wide, wide), BF16)] * 5
                        + [pltpu.VMEM((seq + S5_C2 * S5_PAD, LANE), F32)]
                        + [pltpu.VMEM((rows, wide), F32)] * 4),
        compiler_params=_cparams(("parallel", "arbitrary")),
        name="s5_branch",
    )(u, bx, cax, cw, pw, pw, d)


def _s5_prep(fwd, bwd):
    G, P, C, L = S5_GROUPS, S5_STATE, S5_GROUP, S5_L
    nt = G // 8
    a_re, a_im, log_dt, b_re, b_im, c_re, c_im = (jnp.stack([f, b]) for f, b in zip(fwd, bwd))
    lam_re = jnp.minimum(a_re, -1e-4)
    lam_im = a_im
    dt = jnp.exp(log_dt)[..., None]
    mag = jnp.exp(lam_re * dt)
    lb_re = mag * jnp.cos(lam_im * dt)
    lb_im = mag * jnp.sin(lam_im * dt)
    den = lam_re * lam_re + lam_im * lam_im
    nr = lb_re - 1.0
    f_re = (nr * lam_re + lb_im * lam_im) / den
    f_im = (lb_im * lam_re - nr * lam_im) / den
    bt_re = jnp.swapaxes(b_re, -1, -2)
    bt_im = jnp.swapaxes(b_im, -1, -2)
    bbt_re = f_re[:, :, None, :] * bt_re - f_im[:, :, None, :] * bt_im
    bbt_im = f_re[:, :, None, :] * bt_im + f_im[:, :, None, :] * bt_re

    def power(n):
        n = jnp.asarray(n, F32).reshape(-1, 1, 1, 1)
        m = jnp.exp(n * (lam_re * dt))
        ang = n * (lam_im * dt)
        return m * jnp.cos(ang), m * jnp.sin(ang)

    pr, pi = power(jnp.arange(L + 1))

    pr2 = jnp.concatenate([pr, -pi], axis=-1)[:, :, :, None, :]
    pi2 = jnp.concatenate([-pi, -pr], axis=-1)[:, :, :, None, :]
    c_ri = jnp.concatenate([c_re, c_re], axis=-1)[None]
    c_ii = jnp.concatenate([c_im, c_im], axis=-1)[None]
    cax = (c_ri * pr2 + c_ii * pi2).reshape(L + 1, 2, nt, 8 * C, 2 * P)
    bx = jnp.concatenate([bbt_re, bbt_im], axis=-1)

    prw = jnp.stack([pr[L - 1::-1, 0], pr[:L, 1]], axis=1)
    piw = jnp.stack([pi[L - 1::-1, 0], pi[:L, 1]], axis=1)
    aa = jnp.concatenate([prw, prw], axis=-1)[:, :, :, None, :]
    ab = jnp.concatenate([-piw, piw], axis=-1)[:, :, :, None, :]
    x2 = jnp.concatenate([bbt_im, bbt_re], axis=-1)[None]
    cw = (aa * bx[None] + ab * x2).astype(BF16).reshape(L, 2, nt, 8 * C, 2 * P)
    bx = bx.reshape(2, nt, 8 * C, 2 * P)

    ns = ([8.0, 64.0, 512.0, 1024.0, 2048.0] + [64.0 * (k + 1) for k in range(8)]
          + [8.0 * k for k in range(8)] + [0.0] * (PW_ROWS - 21))
    qr, qi = power(ns)
    pw = jnp.concatenate([qr.reshape(PW_ROWS, 2, nt, 8 * P), qi.reshape(PW_ROWS, 2, nt, 8 * P)],
                         axis=-1).reshape(PW_ROWS, 2 * nt * 2 * 8 * P)
    return bx, cax, cw, pw


GLU_COLS = 256


def _glu_mm_body(a_ref, w_ref, g_ref, o_ref):
    a = a_ref[...]
    n = o_ref.shape[1]
    for c0 in range(0, n, GLU_COLS):
        z1 = jnp.dot(a, w_ref[:, c0:c0 + GLU_COLS], preferred_element_type=F32)
        z2 = jnp.dot(a, w_ref[:, n + c0:n + c0 + GLU_COLS], preferred_element_type=F32)
        gate = _silu(g_ref[:, c0:c0 + GLU_COLS].astype(F32))
        o_ref[:, c0:c0 + GLU_COLS] = (z1 * jax.nn.sigmoid(z2) * gate).astype(o_ref.dtype)


def _glu_matmul(a, w, h2, tm=1024):
    t, k = a.shape
    n = w.shape[1] // 2
    return pl.pallas_call(
        _glu_mm_body,
        grid=(t // tm,),
        in_specs=[pl.BlockSpec((tm, k), lambda i: (i, 0)),
                  pl.BlockSpec((k, 2 * n), lambda i: (0, 0)),
                  pl.BlockSpec((tm, n), lambda i: (i, O_GATE // n))],
        out_specs=pl.BlockSpec((tm, n), lambda i: (i, 0)),
        out_shape=jax.ShapeDtypeStruct((t, n), BF16),
        compiler_params=_cparams(("parallel",)),
        name="s5_glu",
    )(a, w, h2)


def _even_in_weight(w_in):
    w = w_in.astype(BF16)
    k = w.shape[0]
    k_rope = w[:, E_HEAD:E_HEAD + QK_ROPE]
    z = jnp.zeros((k, LANE - QK_ROPE), BF16)
    return w, jnp.concatenate([w[:, E_HEAD + QK_ROPE:], k_rope, z], axis=1)


def _uq_weight(w_uq):
    k = w_uq.shape[0]
    w = w_uq.reshape(k, MLA_HEADS, QK_NOPE + QK_ROPE)
    nope = w[:, :, :QK_NOPE].reshape(k, MLA_HEADS * QK_NOPE)
    rope = w[:, :, QK_NOPE:].reshape(k, MLA_HEADS * QK_ROPE)
    return jnp.concatenate([nope, rope], axis=1).astype(BF16)


def _ukv_weight(w_ukv):
    k = w_ukv.shape[0]
    w = w_ukv.reshape(k, MLA_HEADS, QK_NOPE + V_DIM)
    kn = w[:, :, :QK_NOPE].reshape(k, MLA_HEADS * QK_NOPE)
    vv = w[:, :, QK_NOPE:].reshape(k, MLA_HEADS * V_DIM)
    return jnp.concatenate([kn, vv], axis=1).astype(BF16)


def _rope_tables(positions):
    bsz, seq = positions.shape
    inv_freq = ROPE_THETA ** (-jnp.arange(0, QK_ROPE, 2, dtype=F32) / QK_ROPE)
    half = QK_ROPE // 2
    inv4 = jnp.tile(inv_freq, LANE // half)
    sign = jnp.tile(jnp.concatenate([-jnp.ones((half,), F32), jnp.ones((half,), F32)]),
                    LANE // QK_ROPE)
    pos = positions.astype(F32).reshape(bsz, seq // ROPE_PACK, ROPE_PACK)
    ang = jnp.repeat(pos, half, axis=-1) * inv4
    return jnp.cos(ang), jnp.sin(ang), sign.reshape(1, LANE)


def _row(v):
    return v.reshape(1, -1)


def _even_layer(x2, mem2, cos128, sin128, sign, bsz, seq, w_in, conv_w, conv_b, conv_ln_g, conv_ln_b,
                q_norm, w_uq, kv_norm, w_ukv, w_mem_kv, w_out, ln_g, ln_b):
    taps = jnp.broadcast_to(conv_w.reshape(CONV_WIDTH, 1, CONV_DIM), (CONV_WIDTH, SUBLANE, CONV_DIM))
    h, a_out = _in_proj_conv(x2, *_even_in_weight(w_in), taps, _row(conv_b), _row(conv_ln_g),
                             _row(conv_ln_b), seq)
    q, k, v = _mla_proj(h, cos128, sin128, sign, _row(q_norm), _row(kv_norm), _uq_weight(w_uq),
                        _ukv_weight(w_ukv), bsz, seq)
    b_out = _mla_attention(q, k, v, h)
    (memkv,) = _proj(mem2, w_mem_kv.astype(BF16), [(2 * MEM_DIM, BF16)], 512, "even_mem_kv")
    memkv = memkv.reshape(bsz, -1, 2 * MEM_DIM)
    return _out_ln([a_out, b_out], h, memkv, E_MEM_Q, E_MEM_GATE, seq, w_out.astype(BF16), x2,
                   _row(ln_g), _row(ln_b), name="even_out_ln")


def _odd_layer(x2, mem2, bsz, seq, w_in, s5_fwd, s5_bwd, s5_d, w_glu, w_mem_kv, w_out, ln_g, ln_b):
    n_in = w_in.shape[1]
    u, h2 = _proj(x2, w_in.astype(BF16), [(O_SPLIT, F32), (n_in - O_SPLIT, BF16)], 1024,
                  "odd_in_proj")
    g = _s5_branch(u, _s5_prep(s5_fwd, s5_bwd), _row(s5_d), bsz, seq)
    c_out = _glu_matmul(g, w_glu.astype(BF16), h2)
    (memkv,) = _proj(mem2, w_mem_kv.astype(BF16), [(2 * MEM_DIM, BF16)], 512, "odd_mem_kv")
    memkv = memkv.reshape(bsz, -1, 2 * MEM_DIM)
    return _out_ln([c_out], h2, memkv, O_MEM_Q, O_MEM_GATE, seq, w_out.astype(BF16), x2,
                   _row(ln_g), _row(ln_b), name="odd_out_ln")


def kernel(x, mem, positions, e_w_in, e_conv_w, e_conv_b, e_conv_ln_g, e_conv_ln_b, e_q_norm, e_w_uq, e_kv_norm, e_w_ukv, e_mem_kv, e_w_out, e_ln_g, e_ln_b, o_w_in, o_a_re_f, o_a_im_f, o_log_dt_f, o_b_re_f, o_b_im_f, o_c_re_f, o_c_im_f, o_a_re_b, o_a_im_b, o_log_dt_b, o_b_re_b, o_b_im_b, o_c_re_b, o_c_im_b, o_d, o_w_glu, o_mem_kv, o_w_out, o_ln_g, o_ln_b):
    bsz, seq, d = x.shape
    cos128, sin128, sign = _rope_tables(positions)
    x2 = x.reshape(bsz * seq, d)
    mem2 = mem.reshape(-1, d)
    h = _even_layer(x2, mem2, cos128, sin128, sign, bsz, seq, e_w_in[0], e_conv_w[0], e_conv_b[0],
                    e_conv_ln_g[0], e_conv_ln_b[0], e_q_norm[0], e_w_uq[0], e_kv_norm[0],
                    e_w_ukv[0], e_mem_kv[0], e_w_out[0], e_ln_g[0], e_ln_b[0])
    s5_fwd = (o_a_re_f[0], o_a_im_f[0], o_log_dt_f[0], o_b_re_f[0], o_b_im_f[0], o_c_re_f[0], o_c_im_f[0])
    s5_bwd = (o_a_re_b[0], o_a_im_b[0], o_log_dt_b[0], o_b_re_b[0], o_b_im_b[0], o_c_re_b[0], o_c_im_b[0])
    h = _odd_layer(h, mem2, bsz, seq, o_w_in[0], s5_fwd, s5_bwd, o_d[0], o_w_glu[0], o_mem_kv[0],
                   o_w_out[0], o_ln_g[0], o_ln_b[0])
    return h.reshape(bsz, seq, d)
```

```python
import functools
import math

import jax
import jax.numpy as jnp
from jax import lax
from jax.experimental import pallas as pl
from jax.experimental.pallas import tpu as pltpu

F32 = jnp.float32
BF16 = jnp.bfloat16

D_MODEL = 1024
CONV_DIM = 1024
CONV_WIDTH = 31
CONV_PAD = 15
MLA_HEADS = 8
QK_NOPE = 128
QK_ROPE = 64
V_DIM = 128
Q_LORA = 768
KV_LORA = 256
ROPE_THETA = 10000.0
MEM_HEADS = 4
MEM_HEAD_DIM = 128
MEM_DIM = 512
S5_DIM = 1024
S5_GROUP = 16
S5_GROUPS = 64
S5_STATE = 64
LN_EPS = 1e-5
RMS_EPS = 1e-6
DEPTH = 2
ALPHA = (2 * DEPTH) ** 0.25

LANE = 128
SUBLANE = 8
VMEM_LIMIT = 56 * 1024 * 1024

E_CONV = 3 * CONV_DIM
E_CQ = 0
E_CKV = 768
E_MLA_GATE = 1024
E_MEM_Q = 2048
E_MEM_GATE = 2560
E_KROPE = 3072
E_NREST = 3200
E_HEAD = E_CONV + Q_LORA + KV_LORA
O_SPLIT = 1024
O_GATE = 0
O_MEM_Q = 1024
O_MEM_GATE = 1536

S5_L = 8


def _cparams(sem):
    return pltpu.CompilerParams(dimension_semantics=sem, vmem_limit_bytes=VMEM_LIMIT)


def _silu(x):
    return x * jax.nn.sigmoid(x)


def _proj_body(a_ref, w_ref, *o_refs, bounds):
    y = jnp.dot(a_ref[...].astype(BF16), w_ref[...], preferred_element_type=F32)
    for o_ref, (lo, hi) in zip(o_refs, bounds):
        o_ref[...] = y[:, lo:hi].astype(o_ref.dtype)


def _proj(a, w, outs, tm, name):
    m, k = a.shape
    n = w.shape[1]
    tm = min(tm, m)
    assert m % tm == 0 and sum(width for width, _ in outs) == n
    bounds, lo = [], 0
    for width, _ in outs:
        bounds.append((lo, lo + width))
        lo += width
    res = pl.pallas_call(
        functools.partial(_proj_body, bounds=tuple(bounds)),
        grid=(m // tm,),
        in_specs=[pl.BlockSpec((tm, k), lambda i: (i, 0)),
                  pl.BlockSpec((k, n), lambda i: (0, 0))],
        out_specs=[pl.BlockSpec((tm, width), lambda i: (i, 0)) for width, _ in outs],
        out_shape=[jax.ShapeDtypeStruct((m, width), dt) for width, dt in outs],
        compiler_params=_cparams(("parallel",)),
        name=name,
    )(a, w)
    return res


CONV_HALO = 16
CONV_RC = 32
CONV_COLS = 256


def _in_conv_body(x_ref, wa_ref, wb_ref, cw, cb, lg, lb, h_ref, o_ref, xb, slab, cur, gcur, gate, tail,
                  *, ts, tiles_per_seq):
    i = pl.program_id(0)

    @pl.when(i == 0)
    def _():
        cur[...] = jnp.zeros_like(cur)
        gcur[...] = jnp.zeros_like(gcur)
        tail[...] = jnp.zeros_like(tail)

    xb[...] = x_ref[...].astype(BF16)
    na = wa_ref.shape[1]
    n = na + wb_ref.shape[1]

    def proj(lo, hi):
        assert hi <= na or lo >= na
        w = wa_ref[:, lo:hi] if hi <= na else wb_ref[:, lo - na:hi - na]
        return jnp.dot(xb[...], w, preferred_element_type=F32)

    glu_new = proj(0, CONV_DIM) * jax.nn.sigmoid(proj(CONV_DIM, 2 * CONV_DIM))

    j = i - 1
    first = (j % tiles_per_seq) == 0
    last = (j % tiles_per_seq) == tiles_per_seq - 1
    slab[0, pl.ds(CONV_HALO, ts), :] = cur[...]
    slab[0, pl.ds(0, CONV_HALO), :] = jnp.where(first, 0.0, tail[...])
    slab[0, pl.ds(CONV_HALO + ts, CONV_HALO), :] = jnp.where(last, 0.0, glu_new[0:CONV_HALO, :])
    gate[...] = gcur[...]
    tail[...] = cur[pl.ds(ts - CONV_HALO, CONV_HALO), :]
    cur[...] = glu_new

    rows = ts + 2 * CONV_HALO
    for lt in range(CONV_DIM // LANE):
        cols = slice(lt * LANE, (lt + 1) * LANE)
        base = slab[0, :, cols]
        for r in range(1, SUBLANE):
            slab[r, :, cols] = pltpu.roll(base, rows - r, 0)

    nchunks = ts // CONV_RC
    step_cols = (n - 2 * CONV_DIM) // nchunks // LANE * LANE
    off = CONV_HALO - CONV_PAD
    for rc in range(nchunks):
        lo = 2 * CONV_DIM + rc * step_cols
        hi = n if rc == nchunks - 1 else lo + step_cols
        yp = proj(lo, hi)
        if hi <= E_CONV:
            gcur[:, lo - 2 * CONV_DIM:hi - 2 * CONV_DIM] = yp.astype(gcur.dtype)
        else:
            assert lo >= E_CONV
            h_ref[:, lo - E_CONV:hi - E_CONV] = yp.astype(h_ref.dtype)

        base = rc * CONV_RC
        halves = []
        for c0 in range(0, CONV_DIM, CONV_COLS):
            cols = slice(c0, c0 + CONV_COLS)
            acc = jnp.zeros((CONV_RC, CONV_COLS), F32)
            for k in range(CONV_WIDTH):
                r = (off + k) % SUBLANE
                wk = jnp.concatenate([cw[k, :, cols]] * (CONV_RC // SUBLANE), axis=0)
                acc = acc + slab[r, pl.ds(base + off + k - r, CONV_RC), cols] * wk
            halves.append(acc)
        dw = jnp.concatenate(halves, axis=1) + cb[...]
        mu = jnp.mean(dw, axis=-1, keepdims=True)
        xc = dw - mu
        var = jnp.mean(xc * xc, axis=-1, keepdims=True)
        yn = xc * lax.rsqrt(var + LN_EPS) * lg[...] + lb[...]
        res = _silu(yn) * _silu(gate[pl.ds(base, CONV_RC), :].astype(F32))
        o_ref[pl.ds(base, CONV_RC), :] = res.astype(o_ref.dtype)


def _in_proj_conv(x, w_full, w_tail, taps, conv_b, ln_g, ln_b, seq, ts=256):
    t, k = x.shape
    n = E_HEAD + w_tail.shape[1]
    nt = t // ts
    body = functools.partial(_in_conv_body, ts=ts, tiles_per_seq=seq // ts)
    vec = pl.BlockSpec((1, CONV_DIM), lambda i: (0, 0))
    return pl.pallas_call(
        body,
        grid=(nt + 1,),
        in_specs=[
            pl.BlockSpec((ts, k), lambda i: (jnp.minimum(i, nt - 1), 0)),
            pl.BlockSpec((k, E_HEAD), lambda i: (0, 0)),
            pl.BlockSpec(w_tail.shape, lambda i: (0, 0)),
            pl.BlockSpec((CONV_WIDTH, SUBLANE, CONV_DIM), lambda i: (0, 0, 0)),
            vec, vec, vec,
        ],
        out_specs=[pl.BlockSpec((ts, n - E_CONV), lambda i: (jnp.minimum(i, nt - 1), 0)),
                   pl.BlockSpec((ts, CONV_DIM), lambda i: (jnp.maximum(i - 1, 0), 0))],
        out_shape=[jax.ShapeDtypeStruct((t, n - E_CONV), BF16),
                   jax.ShapeDtypeStruct((t, CONV_DIM), BF16)],
        scratch_shapes=[pltpu.VMEM((ts, k), BF16),
                        pltpu.VMEM((SUBLANE, ts + 2 * CONV_HALO, CONV_DIM), F32),
                        pltpu.VMEM((ts, CONV_DIM), F32),
                        pltpu.VMEM((ts, CONV_DIM), BF16),
                        pltpu.VMEM((ts, CONV_DIM), BF16),
                        pltpu.VMEM((CONV_HALO, CONV_DIM), F32)],
        compiler_params=_cparams(("arbitrary",)),
        name="in_proj_conv",
    )(x, w_full, w_tail, taps, conv_b, ln_g, ln_b)


def _rms(x, g):
    ms = jnp.mean(x * x, axis=-1, keepdims=True)
    return x * lax.rsqrt(ms + RMS_EPS) * g


MLA_PROJ_CHUNKS = 2
ROPE_PACK = LANE // (QK_ROPE // 2)


def _expand_rope_table(tbl_ref, r0, rc, out_ref, sign):
    half = QK_ROPE // 2
    tbl = tbl_ref[0, pl.ds(r0 // ROPE_PACK, rc // ROPE_PACK), :]
    lane = lax.broadcasted_iota(jnp.int32, tbl.shape, 1)
    for q in range(ROPE_PACK):
        seg = tbl if q == 0 else pltpu.roll(tbl, LANE - q * half, 1)
        m = jnp.where(lane < half, seg, 0.0)
        m = m + pltpu.roll(m, half, 1)
        m = m + pltpu.roll(m, 2 * half, 1)
        if sign is not None:
            m = m * sign
        out_ref[pl.ds(q, rc // ROPE_PACK, stride=ROPE_PACK), :] = m
    return out_ref[...]


def _rope(x, c, s):
    half = QK_ROPE // 2
    lane = lax.broadcasted_iota(jnp.int32, x.shape, 1)
    swapped = jnp.where((lane & (QK_ROPE - 1)) < half,
                        pltpu.roll(x, LANE - half, 1),
                        pltpu.roll(x, half, 1))
    return x * c + swapped * s


def _mla_proj_body(cq, ckv, kr, cos, sin, sign, qn_g, kvn_g, wq, wkv, q_ref, k_ref, v_ref, cexp, sexp,
                   *, scale):
    nh = MLA_HEADS * QK_NOPE
    nr = MLA_HEADS * QK_ROPE
    rc = cq.shape[0] // MLA_PROJ_CHUNKS
    for r0 in range(0, cq.shape[0], rc):
        rows = pl.ds(r0, rc)
        c = _expand_rope_table(cos, r0, rc, cexp, None)
        s = _expand_rope_table(sin, r0, rc, sexp, sign[...])
        nq = _rms(cq[rows, :].astype(F32), qn_g[...]).astype(BF16)
        qf = jnp.dot(nq, wq[...], preferred_element_type=F32)
        qr = [_rope(qf[:, nh + p * LANE:nh + (p + 1) * LANE], c, s) for p in range(nr // LANE)]
        nkv = _rms(ckv[rows, :].astype(F32), kvn_g[...]).astype(BF16)
        kvf = jnp.dot(nkv, wkv[...], preferred_element_type=F32)
        kr_even = _rope(kr[rows, :].astype(F32), c, s)
        kr_odd = pltpu.roll(kr_even, QK_ROPE, 1)
        for h in range(MLA_HEADS):
            q_ref[0, h, rows, 0:QK_NOPE] = (qf[:, h * QK_NOPE:(h + 1) * QK_NOPE] * scale).astype(BF16)
            q_ref[0, h, rows, QK_NOPE:2 * QK_NOPE] = (qr[h // 2] * scale).astype(BF16)
            k_ref[0, h, rows, 0:QK_NOPE] = kvf[:, h * QK_NOPE:(h + 1) * QK_NOPE].astype(BF16)
            k_ref[0, h, rows, QK_NOPE:2 * QK_NOPE] = (kr_even if h % 2 == 0 else kr_odd).astype(BF16)
            v_ref[0, h, rows, :] = kvf[:, nh + h * V_DIM:nh + (h + 1) * V_DIM].astype(BF16)


def _mla_proj(h, cos128, sin128, sign, q_norm, kv_norm, wq, wkv, bsz, seq, tm=1024):
    nt = seq // tm
    scale = (QK_NOPE + QK_ROPE) ** -0.5 * math.log2(math.e)
    body = functools.partial(_mla_proj_body, scale=scale)
    hd = 2 * QK_NOPE
    return pl.pallas_call(
        body,
        grid=(bsz, nt),
        in_specs=[
            pl.BlockSpec((tm, Q_LORA), lambda b, i: (b * nt + i, E_CQ // Q_LORA)),
            pl.BlockSpec((tm, KV_LORA), lambda b, i: (b * nt + i, E_CKV // KV_LORA)),
            pl.BlockSpec((tm, LANE), lambda b, i: (b * nt + i, E_KROPE // LANE)),
            pl.BlockSpec((1, tm // ROPE_PACK, LANE), lambda b, i: (b, i, 0)),
            pl.BlockSpec((1, tm // ROPE_PACK, LANE), lambda b, i: (b, i, 0)),
            pl.BlockSpec((1, LANE), lambda b, i: (0, 0)),
            pl.BlockSpec((1, Q_LORA), lambda b, i: (0, 0)),
            pl.BlockSpec((1, KV_LORA), lambda b, i: (0, 0)),
            pl.BlockSpec(wq.shape, lambda b, i: (0, 0)),
            pl.BlockSpec(wkv.shape, lambda b, i: (0, 0)),
        ],
        out_specs=[
            pl.BlockSpec((1, MLA_HEADS, tm, hd), lambda b, i: (b, 0, i, 0)),
            pl.BlockSpec((1, MLA_HEADS, tm, hd), lambda b, i: (b, 0, i, 0)),
            pl.BlockSpec((1, MLA_HEADS, tm, V_DIM), lambda b, i: (b, 0, i, 0)),
        ],
        out_shape=[
            jax.ShapeDtypeStruct((bsz, MLA_HEADS, seq, hd), BF16),
            jax.ShapeDtypeStruct((bsz, MLA_HEADS, seq, hd), BF16),
            jax.ShapeDtypeStruct((bsz, MLA_HEADS, seq, V_DIM), BF16),
        ],
        scratch_shapes=[pltpu.VMEM((tm // MLA_PROJ_CHUNKS, LANE), F32)] * 2,
        compiler_params=_cparams(("parallel", "parallel")),
        name="mla_proj",
    )(h, h, h, cos128, sin128, sign, q_norm, kv_norm, wq, wkv)


def _flash_body(q_ref, k_ref, v_ref, g_ref, o_ref, *, tk, nk, unroll):
    q = q_ref[0, 0]
    tq = q.shape[0]

    ones = jnp.ones((tk, V_DIM), BF16)

    def step(c, carry):
        m, acc = carry
        start = pl.multiple_of(c * tk, tk)
        ks = k_ref[0, 0, pl.ds(start, tk), :]
        vs = jnp.concatenate([v_ref[0, 0, pl.ds(start, tk), :], ones], axis=1)
        s = lax.dot_general(q, ks, (((1,), (1,)), ((), ())), preferred_element_type=F32)
        m_new = jnp.maximum(m, jnp.max(s, axis=-1, keepdims=True))
        alpha = jnp.exp2(m - m_new)
        p = jnp.exp2(s - m_new)
        acc = alpha * acc + jnp.dot(p.astype(BF16), vs, preferred_element_type=F32)
        return m_new, acc

    m0 = jnp.full((tq, 1), -jnp.inf, F32)
    a0 = jnp.zeros((tq, 2 * V_DIM), F32)
    m, acc = lax.fori_loop(0, nk, step, (m0, a0), unroll=unroll)
    o_ref[...] = (acc[:, :V_DIM] / acc[:, V_DIM:] * _silu(g_ref[...].astype(F32))).astype(o_ref.dtype)


def _mla_attention(q, k, v, h, tq=2048, tk=256, unroll=True):
    bsz, heads, seq, hd = q.shape
    nq = seq // tq
    body = functools.partial(_flash_body, tk=tk, nk=seq // tk, unroll=unroll)
    gcol = E_MLA_GATE // V_DIM
    return pl.pallas_call(
        body,
        grid=(bsz, heads, nq),
        in_specs=[
            pl.BlockSpec((1, 1, tq, hd), lambda b, hh, i: (b, hh, i, 0)),
            pl.BlockSpec((1, 1, seq, hd), lambda b, hh, i: (b, hh, 0, 0)),
            pl.BlockSpec((1, 1, seq, V_DIM), lambda b, hh, i: (b, hh, 0, 0)),
            pl.BlockSpec((tq, V_DIM), lambda b, hh, i: (b * nq + i, gcol + hh)),
        ],
        out_specs=pl.BlockSpec((tq, V_DIM), lambda b, hh, i: (b * nq + i, hh)),
        out_shape=jax.ShapeDtypeStruct((bsz * seq, heads * V_DIM), BF16),
        compiler_params=_cparams(("parallel", "parallel", "arbitrary")),
        name="mla_attention",
    )(q, k, v, h)


OUT_LN_CHUNKS = 4


def _mem_attn(q_ref, g_ref, kv_ref, rows, scale):
    ones = jnp.ones((kv_ref.shape[1], MEM_HEAD_DIM), BF16)
    outs = []
    for hh in range(MEM_HEADS):
        lo, hi = hh * MEM_HEAD_DIM, (hh + 1) * MEM_HEAD_DIM
        kk = kv_ref[0, :, lo:hi]
        vv = jnp.concatenate([kv_ref[0, :, MEM_DIM + lo:MEM_DIM + hi], ones], axis=1)
        s = lax.dot_general(q_ref[rows, lo:hi], kk, (((1,), (1,)), ((), ())),
                            preferred_element_type=F32) * scale
        m = jnp.max(s, axis=-1, keepdims=True)
        p = jnp.exp2(s - m)
        o = jnp.dot(p.astype(BF16), vv, preferred_element_type=F32)
        o = o[:, :MEM_HEAD_DIM] / o[:, MEM_HEAD_DIM:]
        outs.append((o * _silu(g_ref[rows, lo:hi].astype(F32))).astype(BF16))
    return jnp.concatenate(outs, axis=1)


def _out_ln_body(*refs, nparts, scale):
    parts = refs[:nparts]
    q_ref, mg_ref, kv_ref, w_ref, x_ref, g_ref, b_ref, o_ref = refs[nparts:]
    rc = o_ref.shape[0] // OUT_LN_CHUNKS
    for r0 in range(0, o_ref.shape[0], rc):
        rows = pl.ds(r0, rc)
        m_out = _mem_attn(q_ref, mg_ref, kv_ref, rows, scale)
        cat = jnp.concatenate([p[rows, :] for p in parts] + [m_out], axis=1)
        y = jnp.dot(cat, w_ref[...], preferred_element_type=F32)
        z = ALPHA * x_ref[rows, :] + y
        mu = jnp.mean(z, axis=-1, keepdims=True)
        zc = z - mu
        var = jnp.mean(zc * zc, axis=-1, keepdims=True)
        o_ref[rows, :] = zc * lax.rsqrt(var + LN_EPS) * g_ref[...] + b_ref[...]


def _out_ln(parts, h, memkv, q_off, g_off, seq, w, x, g, b, tm=1024, name="out_ln"):
    t = x.shape[0]
    n = len(parts)
    nt = seq // tm
    mlen = memkv.shape[1]
    body = functools.partial(_out_ln_body, nparts=n, scale=MEM_HEAD_DIM ** -0.5 * math.log2(math.e))
    assert sum(p.shape[1] for p in parts) + MEM_DIM == w.shape[0]
    in_specs = [pl.BlockSpec((tm, p.shape[1]), lambda i: (i, 0)) for p in parts]
    in_specs += [pl.BlockSpec((tm, MEM_DIM), lambda i: (i, q_off // MEM_DIM)),
                 pl.BlockSpec((tm, MEM_DIM), lambda i: (i, g_off // MEM_DIM)),
                 pl.BlockSpec((1, mlen, 2 * MEM_DIM), lambda i: (i // nt, 0, 0)),
                 pl.BlockSpec(w.shape, lambda i: (0, 0)),
                 pl.BlockSpec((tm, D_MODEL), lambda i: (i, 0)),
                 pl.BlockSpec((1, D_MODEL), lambda i: (0, 0)),
                 pl.BlockSpec((1, D_MODEL), lambda i: (0, 0))]
    return pl.pallas_call(
        body,
        grid=(t // tm,),
        in_specs=in_specs,
        out_specs=pl.BlockSpec((tm, D_MODEL), lambda i: (i, 0)),
        out_shape=jax.ShapeDtypeStruct((t, D_MODEL), F32),
        compiler_params=_cparams(("parallel",)),
        name=name,
    )(*parts, h, h, memkv, w, x, g, b)


S5_C2 = 8
PW_A8, PW_A64, PW_A512, PW_A1024, PW_A2048 = 0, 1, 2, 3, 4
PW_A64K = 5
PW_A8K = 13
PW_ROWS = 24
S5_PAD = 8


def _cmul(pr, pi, xr, xi):
    return pr * xr - pi * xi, pr * xi + pi * xr


def _s5_scan(s_ref, x_ref, pw_ref, rev):
    half = 512
    n0 = 8 * S5_C2
    n1 = S5_C2

    def pw(row):
        v = pw_ref[pl.ds(row, 1), :]
        return v[:, :half], v[:, half:]

    def blk0(k):
        return (7 - k if rev else k) * n0

    def blk1(k):
        return (7 - k if rev else k) * n1

    def ld(ref, start, n):
        v = ref[pl.ds(start, n), :]
        return v[:, :half], v[:, half:]

    def st(ref, start, n, re, im):
        ref[pl.ds(start, n), 0:half] = re
        ref[pl.ds(start, n), half:2 * half] = im

    a8r, a8i = pw(PW_A8)
    for k in range(1, 8):
        pr_, pi_ = ld(s_ref, blk0(k - 1), n0)
        cr, ci = ld(s_ref, blk0(k), n0)
        mr, mi = _cmul(a8r, a8i, pr_, pi_)
        st(s_ref, blk0(k), n0, cr + mr, ci + mi)
    g0 = blk0(7)
    a64r, a64i = pw(PW_A64)
    for k in range(1, 8):
        pr_, pi_ = ld(s_ref, g0 + blk1(k - 1), n1)
        cr, ci = ld(s_ref, g0 + blk1(k), n1)
        mr, mi = _cmul(a64r, a64i, pr_, pi_)
        st(s_ref, g0 + blk1(k), n1, cr + mr, ci + mi)
    hr, hi = ld(s_ref, g0 + blk1(7), n1)
    c2 = lax.broadcasted_iota(jnp.int32, (n1, half), 0)
    if rev:
        c2 = (S5_C2 - 1) - c2
    for d, row in ((1, PW_A512), (2, PW_A1024), (4, PW_A2048)):
        ar, ai = pw(row)
        sh = (n1 - d) if rev else d
        sr = pltpu.roll(hr, sh, 0)
        si = pltpu.roll(hi, sh, 0)
        mr, mi = _cmul(ar, ai, sr, si)
        keep = c2 >= d
        hr = hr + jnp.where(keep, mr, 0.0)
        hi = hi + jnp.where(keep, mi, 0.0)
    sh1 = (n1 - 1) if rev else 1
    p3r = jnp.where(c2 >= 1, pltpu.roll(hr, sh1, 0), 0.0)
    p3i = jnp.where(c2 >= 1, pltpu.roll(hi, sh1, 0), 0.0)
    for k1 in range(8):
        if k1 == 0:
            er, ei = p3r, p3i
        else:
            ar, ai = pw(PW_A64K + k1 - 1)
            mr, mi = _cmul(ar, ai, p3r, p3i)
            qr, qi = ld(s_ref, g0 + blk1(k1 - 1), n1)
            er, ei = qr + mr, qi + mi
        for k0 in range(8):
            dst = blk0(k0) + blk1(k1)
            if k0 == 0:
                st(x_ref, dst, n1, er, ei)
            else:
                ar, ai = pw(PW_A8K + k0)
                mr, mi = _cmul(ar, ai, er, ei)
                qr, qi = ld(s_ref, blk0(k0 - 1) + blk1(k1), n1)
                st(x_ref, dst, n1, qr + mr, qi + mi)


def _lane_select(d_log2, tt):
    kk = lax.broadcasted_iota(jnp.int32, (LANE, LANE), 0)
    col = tt * LANE + lax.broadcasted_iota(jnp.int32, (LANE, LANE), 1)
    src = lax.shift_left(lax.shift_right_logical(col, d_log2 + 3), d_log2) + (col & ((1 << d_log2) - 1))
    return jnp.where(kk == src, 1.0, 0.0).astype(BF16)


def _expand_blockdiag(comp, m_ref, row0, sels, *, row_b_log2, d_log2):
    rows = comp.shape[0]
    rowg = lax.shift_right_logical(lax.broadcasted_iota(jnp.int32, (rows, LANE), 0), row_b_log2) & 7
    coln = lax.broadcasted_iota(jnp.int32, (rows, LANE), 1)
    for tt in range(8):
        piece = jnp.dot(comp, sels[tt], preferred_element_type=F32)
        colg = lax.shift_right_logical(tt * LANE + coln, d_log2) & 7
        m_ref[pl.ds(row0, rows), tt * LANE:(tt + 1) * LANE] = jnp.where(
            rowg == colg, piece, 0.0).astype(BF16)


_NT = (((1,), (1,)), ((), ()))


def _expand_toeplitz(bx_ref, cax_ref, tw):
    rowg = lax.shift_right_logical(lax.broadcasted_iota(jnp.int32, (LANE, LANE), 0), 4)
    colg = lax.shift_right_logical(lax.broadcasted_iota(jnp.int32, (LANE, LANE), 1), 4)
    keep = rowg == colg

    def kern(n, d):
        return lax.dot_general(bx_ref[d, 0], cax_ref[n, d, 0], _NT,
                               precision=lax.Precision.HIGHEST, preferred_element_type=F32)

    for lag in range(-(S5_L - 1), S5_L):
        if lag == 0:
            k = kern(0, 0) + kern(0, 1)
        else:
            k = kern(lag, 0) if lag > 0 else kern(-lag, 1)
        blk = jnp.where(keep, k, 0.0).astype(BF16)
        for l in range(S5_L):
            lp = l + lag
            if 0 <= lp < S5_L:
                tw[l * LANE:(l + 1) * LANE, lp * LANE:(lp + 1) * LANE] = blk


def _expand_readout(cax_ref, v_ref, d, lags):
    half = 8 * S5_STATE
    row = lax.broadcasted_iota(jnp.int32, (half, LANE), 0)
    lane = lax.broadcasted_iota(jnp.int32, (half, LANE), 1)
    rowg = lax.shift_right_logical(row, 6)
    colg = lax.shift_right_logical(lane, 4)
    for r in range(2):
        pick = jnp.where(((row & (S5_STATE - 1)) + r * S5_STATE) == lane, 1.0, 0.0).astype(BF16)
        for lp, n in enumerate(lags):
            piece = lax.dot_general(pick, cax_ref[n, d, 0].astype(BF16), _NT, preferred_element_type=F32)
            v_ref[r * half:(r + 1) * half, lp * LANE:(lp + 1) * LANE] = jnp.where(
                rowg == colg, piece, 0.0).astype(BF16)


def _s5_body(u_ref, bx_ref, cax_ref, cw_ref, pwf_ref, pwb_ref, d_ref, o_ref,
             tw, wf, wb, vf, vb, stage, xcat, sbuf, xf, xb, *, seq):
    n1 = S5_C2
    blk_rows = seq // S5_C2
    pitch = blk_rows + S5_PAD

    @pl.when(pl.program_id(1) == 0)
    def _():
        _expand_toeplitz(bx_ref, cax_ref, tw)
        sel_w = [_lane_select(6, tt) for tt in range(8)]
        for l in range(S5_L):
            _expand_blockdiag(cw_ref[l, 0, 0], wf, l * LANE, sel_w, row_b_log2=4, d_log2=6)
            _expand_blockdiag(cw_ref[l, 1, 0], wb, l * LANE, sel_w, row_b_log2=4, d_log2=6)
        _expand_readout(cax_ref, vf, 0, [lp + 1 for lp in range(S5_L)])
        _expand_readout(cax_ref, vb, 1, [S5_L - lp for lp in range(S5_L)])

    def pieces():
        for c0 in range(8):
            for c1 in range(8):
                for l in range(S5_L):
                    yield (c0 * 8 * n1 + c1 * n1, l * LANE,
                           pl.ds(c1 * 64 + c0 * 8 + l, n1, stride=pitch))

    for c2 in range(S5_C2):
        stage[pl.ds(c2 * pitch, blk_rows), :] = u_ref[pl.ds(c2 * blk_rows, blk_rows), :]
    for r0, l0, tok in pieces():
        xcat[pl.ds(r0, n1), l0:l0 + LANE] = stage[tok, :]
    xb16 = xcat[...].astype(BF16)
    sbuf[...] = jnp.dot(xb16, wf[...], preferred_element_type=F32)
    _s5_scan(sbuf, xf, pwf_ref, False)
    sbuf[...] = jnp.dot(xb16, wb[...], preferred_element_type=F32)
    _s5_scan(sbuf, xb, pwb_ref, True)
    y = jnp.dot(xb16, tw[...], preferred_element_type=F32)
    y = y + jnp.dot(xf[...].astype(BF16), vf[...], preferred_element_type=F32)
    y = y + jnp.dot(xb[...].astype(BF16), vb[...], preferred_element_type=F32)
    dd = jnp.concatenate([d_ref[...]] * S5_L, axis=-1)
    sbuf[...] = jax.nn.gelu(y + dd * xcat[...])
    for r0, l0, tok in pieces():
        stage[tok, :] = sbuf[pl.ds(r0, n1), l0:l0 + LANE]
    for c2 in range(S5_C2):
        o_ref[pl.ds(c2 * blk_rows, blk_rows), :] = stage[pl.ds(c2 * pitch, blk_rows), :].astype(o_ref.dtype)


def _s5_branch(u, ops, d, bsz, seq):
    t = u.shape[0]
    bx, cax, cw, pw = ops
    ntile = S5_DIM // LANE
    rows = seq // S5_L
    wide = S5_L * LANE
    body = functools.partial(_s5_body, seq=seq)

    def pw_spec(di):
        return pl.BlockSpec((PW_ROWS, wide), lambda j, b: (0, di * ntile + j))

    return pl.pallas_call(
        body,
        grid=(ntile, bsz),
        in_specs=[pl.BlockSpec((seq, LANE), lambda j, b: (b, j)),
                  pl.BlockSpec((2, 1, LANE, LANE), lambda j, b: (0, j, 0, 0)),
                  pl.BlockSpec((S5_L + 1, 2, 1, LANE, LANE), lambda j, b: (0, 0, j, 0, 0)),
                  pl.BlockSpec((S5_L, 2, 1, LANE, LANE), lambda j, b: (0, 0, j, 0, 0)),
                  pw_spec(0), pw_spec(1),
                  pl.BlockSpec((1, LANE), lambda j, b: (0, j))],
        out_specs=pl.BlockSpec((seq, LANE), lambda j, b: (b, j)),
        out_shape=jax.ShapeDtypeStruct((t, S5_DIM), BF16),
        scratch_shapes=([pltpu.VMEM((wide, wide), BF16)] * 5
                        + [pltpu.VMEM((seq + S5_C2 * S5_PAD, LANE), F32)]
                        + [pltpu.VMEM((rows, wide), F32)] * 4),
        compiler_params=_cparams(("parallel", "arbitrary")),
        name="s5_branch",
    )(u, bx, cax, cw, pw, pw, d)


def _s5_prep(fwd, bwd):
    G, P, C, L = S5_GROUPS, S5_STATE, S5_GROUP, S5_L
    nt = G // 8
    a_re, a_im, log_dt, b_re, b_im, c_re, c_im = (jnp.stack([f, b]) for f, b in zip(fwd, bwd))
    lam_re = jnp.minimum(a_re, -1e-4)
    lam_im = a_im
    dt = jnp.exp(log_dt)[..., None]
    mag = jnp.exp(lam_re * dt)
    lb_re = mag * jnp.cos(lam_im * dt)
    lb_im = mag * jnp.sin(lam_im * dt)
    den = lam_re * lam_re + lam_im * lam_im
    nr = lb_re - 1.0
    f_re = (nr * lam_re + lb_im * lam_im) / den
    f_im = (lb_im * lam_re - nr * lam_im) / den
    bt_re = jnp.swapaxes(b_re, -1, -2)
    bt_im = jnp.swapaxes(b_im, -1, -2)
    bbt_re = f_re[:, :, None, :] * bt_re - f_im[:, :, None, :] * bt_im
    bbt_im = f_re[:, :, None, :] * bt_im + f_im[:, :, None, :] * bt_re

    def power(n):
        n = jnp.asarray(n, F32).reshape(-1, 1, 1, 1)
        m = jnp.exp(n * (lam_re * dt))
        ang = n * (lam_im * dt)
        return m * jnp.cos(ang), m * jnp.sin(ang)

    pr, pi = power(jnp.arange(L + 1))

    pr2 = jnp.concatenate([pr, -pi], axis=-1)[:, :, :, None, :]
    pi2 = jnp.concatenate([-pi, -pr], axis=-1)[:, :, :, None, :]
    c_ri = jnp.concatenate([c_re, c_re], axis=-1)[None]
    c_ii = jnp.concatenate([c_im, c_im], axis=-1)[None]
    cax = (c_ri * pr2 + c_ii * pi2).reshape(L + 1, 2, nt, 8 * C, 2 * P)
    bx = jnp.concatenate([bbt_re, bbt_im], axis=-1)

    prw = jnp.stack([pr[L - 1::-1, 0], pr[:L, 1]], axis=1)
    piw = jnp.stack([pi[L - 1::-1, 0], pi[:L, 1]], axis=1)
    aa = jnp.concatenate([prw, prw], axis=-1)[:, :, :, None, :]
    ab = jnp.concatenate([-piw, piw], axis=-1)[:, :, :, None, :]
    x2 = jnp.concatenate([bbt_im, bbt_re], axis=-1)[None]
    cw = (aa * bx[None] + ab * x2).astype(BF16).reshape(L, 2, nt, 8 * C, 2 * P)
    bx = bx.reshape(2, nt, 8 * C, 2 * P)

    ns = ([8.0, 64.0, 512.0, 1024.0, 2048.0] + [64.0 * (k + 1) for k in range(8)]
          + [8.0 * k for k in range(8)] + [0.0] * (PW_ROWS - 21))
    qr, qi = power(ns)
    pw = jnp.concatenate([qr.reshape(PW_ROWS, 2, nt, 8 * P), qi.reshape(PW_ROWS, 2, nt, 8 * P)],
                         axis=-1).reshape(PW_ROWS, 2 * nt * 2 * 8 * P)
    return bx, cax, cw, pw


GLU_COLS = 256


def _glu_mm_body(a_ref, w_ref, g_ref, o_ref):
    a = a_ref[...]
    n = o_ref.shape[1]
    for c0 in range(0, n, GLU_COLS):
        z1 = jnp.dot(a, w_ref[:, c0:c0 + GLU_COLS], preferred_element_type=F32)
        z2 = jnp.dot(a, w_ref[:, n + c0:n + c0 + GLU_COLS], preferred_element_type=F32)
        gate = _silu(g_ref[:, c0:c0 + GLU_COLS].astype(F32))
        o_ref[:, c0:c0 + GLU_COLS] = (z1 * jax.nn.sigmoid(z2) * gate).astype(o_ref.dtype)


def _glu_matmul(a, w, h2, tm=1024):
    t, k = a.shape
    n = w.shape[1] // 2
    return pl.pallas_call(
        _glu_mm_body,
        grid=(t // tm,),
        in_specs=[pl.BlockSpec((tm, k), lambda i: (i, 0)),
                  pl.BlockSpec((k, 2 * n), lambda i: (0, 0)),
                  pl.BlockSpec((tm, n), lambda i: (i, O_GATE // n))],
        out_specs=pl.BlockSpec((tm, n), lambda i: (i, 0)),
        out_shape=jax.ShapeDtypeStruct((t, n), BF16),
        compiler_params=_cparams(("parallel",)),
        name="s5_glu",
    )(a, w, h2)


def _even_in_weight(w_in):
    w = w_in.astype(BF16)
    k = w.shape[0]
    k_rope = w[:, E_HEAD:E_HEAD + QK_ROPE]
    z = jnp.zeros((k, LANE - QK_ROPE), BF16)
    return w, jnp.concatenate([w[:, E_HEAD + QK_ROPE:], k_rope, z], axis=1)


def _uq_weight(w_uq):
    k = w_uq.shape[0]
    w = w_uq.reshape(k, MLA_HEADS, QK_NOPE + QK_ROPE)
    nope = w[:, :, :QK_NOPE].reshape(k, MLA_HEADS * QK_NOPE)
    rope = w[:, :, QK_NOPE:].reshape(k, MLA_HEADS * QK_ROPE)
    return jnp.concatenate([nope, rope], axis=1).astype(BF16)


def _ukv_weight(w_ukv):
    k = w_ukv.shape[0]
    w = w_ukv.reshape(k, MLA_HEADS, QK_NOPE + V_DIM)
    kn = w[:, :, :QK_NOPE].reshape(k, MLA_HEADS * QK_NOPE)
    vv = w[:, :, QK_NOPE:].reshape(k, MLA_HEADS * V_DIM)
    return jnp.concatenate([kn, vv], axis=1).astype(BF16)


def _rope_tables(positions):
    bsz, seq = positions.shape
    inv_freq = ROPE_THETA ** (-jnp.arange(0, QK_ROPE, 2, dtype=F32) / QK_ROPE)
    half = QK_ROPE // 2
    inv4 = jnp.tile(inv_freq, LANE // half)
    sign = jnp.tile(jnp.concatenate([-jnp.ones((half,), F32), jnp.ones((half,), F32)]),
                    LANE // QK_ROPE)
    pos = positions.astype(F32).reshape(bsz, seq // ROPE_PACK, ROPE_PACK)
    ang = jnp.repeat(pos, half, axis=-1) * inv4
    return jnp.cos(ang), jnp.sin(ang), sign.reshape(1, LANE)


def _row(v):
    return v.reshape(1, -1)


def _even_layer(x2, mem2, cos128, sin128, sign, bsz, seq, w_in, conv_w, conv_b, conv_ln_g, conv_ln_b,
                q_norm, w_uq, kv_norm, w_ukv, w_mem_kv, w_out, ln_g, ln_b):
    taps = jnp.broadcast_to(conv_w.reshape(CONV_WIDTH, 1, CONV_DIM), (CONV_WIDTH, SUBLANE, CONV_DIM))
    h, a_out = _in_proj_conv(x2, *_even_in_weight(w_in), taps, _row(conv_b), _row(conv_ln_g),
                             _row(conv_ln_b), seq)
    q, k, v = _mla_proj(h, cos128, sin128, sign, _row(q_norm), _row(kv_norm), _uq_weight(w_uq),
                        _ukv_weight(w_ukv), bsz, seq)
    b_out = _mla_attention(q, k, v, h)
    (memkv,) = _proj(mem2, w_mem_kv.astype(BF16), [(2 * MEM_DIM, BF16)], 512, "even_mem_kv")
    memkv = memkv.reshape(bsz, -1, 2 * MEM_DIM)
    return _out_ln([a_out, b_out], h, memkv, E_MEM_Q, E_MEM_GATE, seq, w_out.astype(BF16), x2,
                   _row(ln_g), _row(ln_b), name="even_out_ln")


def _odd_layer(x2, mem2, bsz, seq, w_in, s5_fwd, s5_bwd, s5_d, w_glu, w_mem_kv, w_out, ln_g, ln_b):
    n_in = w_in.shape[1]
    u, h2 = _proj(x2, w_in.astype(BF16), [(O_SPLIT, F32), (n_in - O_SPLIT, BF16)], 1024,
                  "odd_in_proj")
    g = _s5_branch(u, _s5_prep(s5_fwd, s5_bwd), _row(s5_d), bsz, seq)
    c_out = _glu_matmul(g, w_glu.astype(BF16), h2)
    (memkv,) = _proj(mem2, w_mem_kv.astype(BF16), [(2 * MEM_DIM, BF16)], 512, "odd_mem_kv")
    memkv = memkv.reshape(bsz, -1, 2 * MEM_DIM)
    return _out_ln([c_out], h2, memkv, O_MEM_Q, O_MEM_GATE, seq, w_out.astype(BF16), x2,
                   _row(ln_g), _row(ln_b), name="odd_out_ln")


def kernel(x, mem, positions, e_w_in, e_conv_w, e_conv_b, e_conv_ln_g, e_conv_ln_b, e_q_norm, e_w_uq, e_kv_norm, e_w_ukv, e_mem_kv, e_w_out, e_ln_g, e_ln_b, o_w_in, o_a_re_f, o_a_im_f, o_log_dt_f, o_b_re_f, o_b_im_f, o_c_re_f, o_c_im_f, o_a_re_b, o_a_im_b, o_log_dt_b, o_b_re_b, o_b_im_b, o_c_re_b, o_c_im_b, o_d, o_w_glu, o_mem_kv, o_w_out, o_ln_g, o_ln_b):
    bsz, seq, d = x.shape
    cos128, sin128, sign = _rope_tables(positions)
    x2 = x.reshape(bsz * seq, d)
    mem2 = mem.reshape(-1, d)
    h = _even_layer(x2, mem2, cos128, sin128, sign, bsz, seq, e_w_in[0], e_conv_w[0], e_conv_b[0],
                    e_conv_ln_g[0], e_conv_ln_b[0], e_q_norm[0], e_w_uq[0], e_kv_norm[0],
                    e_w_ukv[0], e_mem_kv[0], e_w_out[0], e_ln_g[0], e_ln_b[0])
    s5_fwd = (o_a_re_f[0], o_a_im_f[0], o_log_dt_f[0], o_b_re_f[0], o_b_im_f[0], o_c_re_f[0], o_c_im_f[0])
    s5_bwd = (o_a_re_b[0], o_a_im_b[0], o_log_dt_b[0], o_b_re_b[0], o_b_im_b[0], o_c_re_b[0], o_c_im_b[0])
    h = _odd_layer(h, mem2, bsz, seq, o_w_in[0], s5_fwd, s5_bwd, o_d[0], o_w_glu[0], o_mem_kv[0],
                   o_w_out[0], o_ln_g[0], o_ln_b[0])
    return h.reshape(bsz, seq, d)
```
